```python
import math
import jax, jax.numpy as jnp
from jax import lax
import numpy as np

D_MODEL = 1024
BATCH = 8
SEQ = 4096
DEPTH = 2
DEC_BATCH = 16
DEC_SEQ = 4096
PAST_LEN = 128

ATT_HEADS = 8
ATT_KV_HEADS = 2
ATT_GROUP = ATT_HEADS // ATT_KV_HEADS
ATT_HEAD_DIM = 64
WINDOW = 128
ATT_BLOCK = 128
ROPE_THETA = 500000.0
ROPE_DIM = ATT_HEAD_DIM // 4
HG_HEADS = 8
HG_KEY_DIM = 64
HG_VAL_DIM = 64
HG_CHUNK = 64
D_FF = 4 * D_MODEL
ALPHA = (2 * DEPTH) ** 0.25
BETA = (8 * DEPTH) ** -0.25
LN_EPS = 1e-5
RMS_EPS = 1e-6

ATT_Q = ATT_HEADS * ATT_HEAD_DIM
ATT_KV = ATT_KV_HEADS * ATT_HEAD_DIM
HG_K = HG_HEADS * HG_KEY_DIM
HG_V = HG_HEADS * HG_VAL_DIM
SPLIT_SIZES = (ATT_Q, ATT_KV, ATT_KV, HG_K, HG_K, HG_K, HG_V, HG_V, D_MODEL, D_MODEL)
D_IN = sum(SPLIT_SIZES)

kernel_name = 'hybrid_gated_swa_hgrn2_encoder'


def layer_norm(x, g, b):
    xf = x.astype(jnp.float32)
    mu = jnp.mean(xf, axis=-1, keepdims=True)
    var = jnp.mean(jnp.square(xf - mu), axis=-1, keepdims=True)
    y = (xf - mu) * lax.rsqrt(var + LN_EPS) * g.astype(jnp.float32) + b.astype(jnp.float32)
    return y.astype(x.dtype)


def partial_rope(x, positions):
    inv = ROPE_THETA ** (-jnp.arange(0, ROPE_DIM, 2, dtype=jnp.float32) / ROPE_DIM)
    ang = positions.astype(jnp.float32)[:, None] * inv[None, :]
    cos = jnp.cos(ang)[None, :, None, :]
    sin = jnp.sin(ang)[None, :, None, :]
    xr = x[..., :ROPE_DIM].astype(jnp.float32)
    x1, x2 = xr[..., :ROPE_DIM // 2], xr[..., ROPE_DIM // 2:]
    rot = jnp.concatenate([x1 * cos - x2 * sin, x2 * cos + x1 * sin], axis=-1).astype(x.dtype)
    return jnp.concatenate([rot, x[..., ROPE_DIM:]], axis=-1)


def window_attention(q, k, v, sink):
    B, S = q.shape[0], q.shape[1]
    nb = S // ATT_BLOCK
    qb = q.reshape(B, nb, ATT_BLOCK, ATT_KV_HEADS, ATT_GROUP, ATT_HEAD_DIM)
    pad = ((0, 0), (ATT_BLOCK, ATT_BLOCK), (0, 0), (0, 0))
    kp = jnp.pad(k, pad).reshape(B, nb + 2, ATT_BLOCK, ATT_KV_HEADS, ATT_HEAD_DIM)
    vp = jnp.pad(v, pad).reshape(B, nb + 2, ATT_BLOCK, ATT_KV_HEADS, ATT_HEAD_DIM)
    kw = jnp.concatenate([kp[:, :-2], kp[:, 1:-1], kp[:, 2:]], axis=2)
    vw = jnp.concatenate([vp[:, :-2], vp[:, 1:-1], vp[:, 2:]], axis=2)
    qi = jnp.arange(ATT_BLOCK)[:, None]
    kj = jnp.arange(3 * ATT_BLOCK)[None, :]
    blk = jnp.arange(nb)[:, None, None]
    kpos = blk * ATT_BLOCK + kj - ATT_BLOCK
    mask = (jnp.abs(kj - ATT_BLOCK - qi) <= WINDOW) & (kpos >= 0) & (kpos < S)
    scale = ATT_HEAD_DIM ** -0.5
    s = jnp.einsum('bnqkgd,bnjkd->bnkgqj', qb, kw).astype(jnp.float32) * scale
    s = jnp.where(mask[None, :, None, None, :, :], s, -jnp.inf)
    sk = sink.astype(jnp.float32).reshape(ATT_KV_HEADS, ATT_GROUP)[None, None, :, :, None, None]
    m = jnp.maximum(jnp.max(s, axis=-1, keepdims=True), sk)
    p = jnp.exp(s - m)
    denom = jnp.sum(p, axis=-1, keepdims=True) + jnp.exp(sk - m)
    p = (p / denom).astype(v.dtype)
    o = jnp.einsum('bnkgqj,bnjkd->bnqkgd', p, vw)
    return o.reshape(B, S, ATT_Q)


def gla_chunk_scan(q, k, v, logf):
    B, L, H, dk = q.shape
    dv = v.shape[-1]
    nc = L // HG_CHUNK

    def to_chunks(a):
        return a.reshape(B, nc, HG_CHUNK, H, a.shape[-1]).transpose(1, 0, 3, 2, 4)

    tri = jnp.arange(HG_CHUNK)[:, None] >= jnp.arange(HG_CHUNK)[None, :]

    def step(S, inp):
        qc, kc, vc, gc = inp
        b = jnp.cumsum(gc, axis=2)
        diff = b[:, :, :, None, :] - b[:, :, None, :, :]
        dec = jnp.exp(jnp.where(tri[:, :, None], diff, -jnp.inf))
        A = jnp.einsum('bhtd,bhsd,bhtsd->bhts', qc, kc, dec)
        o = jnp.einsum('bhts,bhse->bhte', A, vc) + jnp.einsum('bhtd,bhde->bhte', qc * jnp.exp(b), S)
        b_last = b[:, :, -1:, :]
        S = jnp.exp(b_last[:, :, 0, :])[..., None] * S + jnp.einsum('bhsd,bhse->bhde', kc * jnp.exp(b_last - b), vc)
        return S, o

    S0 = jnp.zeros((B, H, dk, dv), jnp.float32)
    _, o = lax.scan(step, S0, (to_chunks(q), to_chunks(k), to_chunks(v), to_chunks(logf)))
    return o.transpose(1, 0, 3, 2, 4).reshape(B, L, H, dv)


def hgrn2_branch(hq, hf_fwd, hf_bwd, hi, hg, lower, norm_g):
    B, L = hq.shape[0], hq.shape[1]
    q = jax.nn.silu(hq.astype(jnp.float32)).reshape(B, L, HG_HEADS, HG_KEY_DIM) * HG_KEY_DIM ** -0.5
    v = hi.astype(jnp.float32).reshape(B, L, HG_HEADS, HG_VAL_DIM)

    def gates(f_pre, lb):
        f = lb + (1.0 - lb) * jax.nn.sigmoid(f_pre.astype(jnp.float32))
        f = f.reshape(B, L, HG_HEADS, HG_KEY_DIM)
        return 1.0 - f, jnp.log(f)

    k_f, g_f = gates(hf_fwd, lower[0])
    k_b, g_b = gates(hf_bwd, lower[1])
    o_f = gla_chunk_scan(q, k_f, v, g_f)
    rev = lambda a: jnp.flip(a, axis=1)
    o_b = rev(gla_chunk_scan(rev(q), rev(k_b), rev(v), rev(g_b)))
    o = o_f + o_b
    o = o * lax.rsqrt(jnp.mean(jnp.square(o), axis=-1, keepdims=True) + RMS_EPS) * norm_g.astype(jnp.float32)
    o = o.reshape(B, L, HG_V) * jax.nn.silu(hg.astype(jnp.float32))
    return o.astype(hq.dtype)


def trunk(x, w_in, att_sink, hgrn_lb, hgrn_norm_g, w_proj_att, w_proj_hgrn, w_out,
          ln1_g, ln1_b, w_ff1, w_ff2, ln2_g, ln2_b):
    B, L, _ = x.shape
    positions = jnp.arange(L)
    sm = jax.nn.softmax(hgrn_lb.astype(jnp.float32), axis=0)
    lower = jnp.cumsum(sm, axis=0) - sm[0:1]
    cuts = [int(c) for c in np.cumsum(SPLIT_SIZES)[:-1]]
    for l in range(DEPTH):
        h = x @ w_in[l]
        aq, ak, av, hq, hf_f, hf_b, hi, hg, ga, gb = jnp.split(h, cuts, axis=-1)
        q = partial_rope(aq.reshape(B, L, ATT_HEADS, ATT_HEAD_DIM), positions)
        k = partial_rope(ak.reshape(B, L, ATT_KV_HEADS, ATT_HEAD_DIM), positions)
        v = av.reshape(B, L, ATT_KV_HEADS, ATT_HEAD_DIM)
        o_att = window_attention(q, k, v, att_sink[l])
        o_hg = hgrn2_branch(hq, hf_f, hf_b, hi, hg, lower[l], hgrn_norm_g[l])
        mixed = jax.nn.sigmoid(ga) * (o_att @ w_proj_att[l]) + jax.nn.sigmoid(gb) * (o_hg @ w_proj_hgrn[l])
        x = layer_norm(ALPHA * x + mixed @ w_out[l], ln1_g[l], ln1_b[l])
        ff = jnp.square(jax.nn.relu(x @ w_ff1[l])) @ w_ff2[l]
        x = layer_norm(ALPHA * x + ff, ln2_g[l], ln2_b[l])
    return x


def setup_inputs(seed: int = 0) -> dict:
    key = jax.random.key(seed)
    ks = jax.random.split(key, 16)
    n = jax.random.normal
    f32 = jnp.float32
    return {
        'x_prompt': n(ks[0], (BATCH, SEQ, D_MODEL), f32),
        'x_sample': n(ks[1], (DEC_BATCH, DEC_SEQ, D_MODEL), f32),
        'w_in': n(ks[2], (DEPTH, D_MODEL, D_IN), f32) * D_MODEL ** -0.5,
        'att_sink': n(ks[3], (DEPTH, ATT_HEADS), f32) * 0.5,
        'hgrn_lb': n(ks[4], (DEPTH, 2, HG_K), f32) * 0.5,
        'hgrn_norm_g': 1.0 + 0.02 * n(ks[5], (DEPTH, HG_VAL_DIM), f32),
        'w_proj_att': n(ks[6], (DEPTH, ATT_Q, D_MODEL), f32) * ATT_Q ** -0.5,
        'w_proj_hgrn': n(ks[7], (DEPTH, HG_V, D_MODEL), f32) * HG_V ** -0.5,
        'w_out': n(ks[8], (DEPTH, D_MODEL, D_MODEL), f32) * (D_MODEL ** -0.5 * BETA),
        'ln1_g': 1.0 + 0.02 * n(ks[9], (DEPTH, D_MODEL), f32),
        'ln1_b': 0.02 * n(ks[10], (DEPTH, D_MODEL), f32),
        'w_ff1': n(ks[11], (DEPTH, D_MODEL, D_FF), f32) * D_MODEL ** -0.5,
        'w_ff2': n(ks[12], (DEPTH, D_FF, D_MODEL), f32) * (D_FF ** -0.5 * BETA),
        'ln2_g': 1.0 + 0.02 * n(ks[13], (DEPTH, D_MODEL), f32),
        'ln2_b': 0.02 * n(ks[14], (DEPTH, D_MODEL), f32),
    }


def reference(x_prompt, x_sample, w_in, att_sink, hgrn_lb, hgrn_norm_g, w_proj_att, w_proj_hgrn,
              w_out, ln1_g, ln1_b, w_ff1, w_ff2, ln2_g, ln2_b):
    y_prompt = trunk(x_prompt, w_in, att_sink, hgrn_lb, hgrn_norm_g, w_proj_att, w_proj_hgrn, w_out,
                     ln1_g, ln1_b, w_ff1, w_ff2, ln2_g, ln2_b)
    y_sample = trunk(x_sample, w_in, att_sink, hgrn_lb, hgrn_norm_g, w_proj_att, w_proj_hgrn, w_out,
                     ln1_g, ln1_b, w_ff1, w_ff2, ln2_g, ln2_b)
    return (y_prompt, y_sample)
```

```python
import functools

import jax
import jax.numpy as jnp
import numpy as np
from jax import lax
from jax.experimental import pallas as pl
from jax.experimental.pallas import tpu as pltpu

F32 = jnp.float32
BF16 = jnp.bfloat16

ATT_HEADS = 8
ATT_KV_HEADS = 2
HEAD_DIM = 64
ATT_BLOCK = 128
ROPE_THETA = 500000.0
ROPE_DIM = HEAD_DIM // 4
HG_HEADS = 8
HG_DIM = 64
DEPTH = 2
ALPHA = (2 * DEPTH) ** 0.25
LN_EPS = 1e-5
RMS_EPS = 1e-6
NEG_BIG = -1e30

LANES = 128
VMEM_LIMIT_BYTES = 56 * 1024 * 1024

PROJ_ROWS = 512
FFN_ROWS = 256
HG_BLOCK = 512
HG_CHUNK = 64
HG_SUB = 16


def _sigmoid(x):
    return 1.0 / (1.0 + jnp.exp(-x))


def _dot(a, b):
    return jnp.dot(a, b, preferred_element_type=F32)


def _dot_tb(a, b):
    return lax.dot_general(a, b, (((1,), (1,)), ((), ())), preferred_element_type=F32)


def _dot_ta(a, b):
    return lax.dot_general(a, b, (((0,), (0,)), ((), ())), preferred_element_type=F32)


def _proj_kernel(x_ref, w_ref, rc_ref, rs1_ref, rs2_ref,
                 q_ref, k_ref, v_ref, hq_ref, ff_ref, fb_ref, hi_ref, hg_ref, ga_ref, gb_ref,
                 *, cuts):
    xb = x_ref[...].astype(BF16)

    def mm(name):
        a, b = cuts[name]
        return _dot(xb, w_ref[:, a:b])

    rc = rc_ref[...]
    rs1 = rs1_ref[...]
    rs2 = rs2_ref[...]

    def rope(blk):
        return blk * rc + pltpu.roll(blk, ROPE_DIM // 2, 1) * rs1 + pltpu.roll(blk, LANES - ROPE_DIM // 2, 1) * rs2

    aq = mm("aq")
    for j in range(aq.shape[1] // LANES):
        blk = rope(aq[:, j * LANES:(j + 1) * LANES]) * (HEAD_DIM ** -0.5)
        q_ref[:, j * LANES:(j + 1) * LANES] = blk.astype(BF16)
    ak = rope(mm("ak"))
    k_ref[:, :LANES] = ak.astype(BF16)
    k_ref[:, LANES:] = pltpu.roll(ak, HEAD_DIM, 1).astype(BF16)
    av = mm("av")
    v_ref[:, :LANES] = av.astype(BF16)
    v_ref[:, LANES:] = pltpu.roll(av, HEAD_DIM, 1).astype(BF16)

    hq = mm("hq")
    hq_ref[...] = hq * _sigmoid(hq) * (HG_DIM ** -0.5)
    ff_ref[...] = mm("hf_f")
    fb_ref[...] = mm("hf_b")
    hi_ref[...] = mm("hi").astype(BF16)
    hg = mm("hg")
    hg_ref[...] = (hg * _sigmoid(hg)).astype(BF16)
    ga_ref[...] = _sigmoid(mm("ga")).astype(BF16)
    gb_ref[...] = _sigmoid(mm("gb")).astype(BF16)


def _proj(x2d, w_in, rope_tabs, seq_len, d_model):
    n_tok = x2d.shape[0]
    tm = PROJ_ROWS
    att_q = ATT_HEADS * HEAD_DIM
    att_kv = ATT_KV_HEADS * HEAD_DIM
    hg = HG_HEADS * HG_DIM
    sizes = [("aq", att_q), ("ak", att_kv), ("av", att_kv), ("hq", hg), ("hf_f", hg), ("hf_b", hg),
             ("hi", hg), ("hg", hg), ("ga", d_model), ("gb", d_model)]
    cuts, off = {}, 0
    for name, s in sizes:
        cuts[name] = (off, off + s)
        off += s
    d_in = off
    pos_blocks = seq_len // tm
    row = lambda c: pl.BlockSpec((tm, c), lambda i: (i, 0))
    tab = pl.BlockSpec((tm, LANES), lambda i: (i % pos_blocks, 0))
    out_shapes = [
        jax.ShapeDtypeStruct((n_tok, att_q), BF16),
        jax.ShapeDtypeStruct((n_tok, 2 * att_kv), BF16),
        jax.ShapeDtypeStruct((n_tok, 2 * att_kv), BF16),
        jax.ShapeDtypeStruct((n_tok, hg), F32),
        jax.ShapeDtypeStruct((n_tok, hg), F32),
        jax.ShapeDtypeStruct((n_tok, hg), F32),
        jax.ShapeDtypeStruct((n_tok, hg), BF16),
        jax.ShapeDtypeStruct((n_tok, hg), BF16),
        jax.ShapeDtypeStruct((n_tok, d_model), BF16),
        jax.ShapeDtypeStruct((n_tok, d_model), BF16),
    ]
    return pl.pallas_call(
        functools.partial(_proj_kernel, cuts=cuts),
        grid=(n_tok // tm,),
        in_specs=[row(d_model),
                  pl.BlockSpec((d_model, d_in), lambda i: (0, 0), pipeline_mode=pl.Buffered(1)),
                  tab, tab, tab],
        out_specs=[row(s.shape[1]) for s in out_shapes],
        out_shape=out_shapes,
        compiler_params=pltpu.CompilerParams(dimension_semantics=("parallel",),
                                             vmem_limit_bytes=VMEM_LIMIT_BYTES),
        name="proj",
    )(x2d, w_in, *rope_tabs)


def _attn_kernel(sink_ref, q_ref, kp_ref, kc_ref, kn_ref, vp_ref, vc_ref, vn_ref, o_ref, *, seq_len):
    blk = ATT_BLOCK
    win = 3 * blk
    i = pl.program_id(1)
    k = jnp.concatenate([kp_ref[...], kc_ref[...], kn_ref[...]], axis=0)
    v = jnp.concatenate([vp_ref[...], vc_ref[...], vn_ref[...]], axis=0)
    lo = lax.broadcasted_iota(jnp.int32, (win, LANES), 1) < HEAD_DIM
    zero = jnp.zeros((win, LANES), BF16)
    k_arr, k_rot = k[:, :LANES], k[:, LANES:]
    v_arr, v_rot = v[:, :LANES], v[:, LANES:]
    k_even = [jnp.where(lo, k_arr, zero), jnp.where(lo, k_rot, zero)]
    k_odd = [jnp.where(lo, zero, k_rot), jnp.where(lo, zero, k_arr)]
    v_even = [jnp.where(lo, v_arr, zero), jnp.where(lo, v_rot, zero)]
    v_odd = [jnp.where(lo, zero, v_rot), jnp.where(lo, zero, v_arr)]

    qi = lax.broadcasted_iota(jnp.int32, (2 * blk, win), 0) % blk
    kj = lax.broadcasted_iota(jnp.int32, (2 * blk, win), 1)
    kpos = i * blk + kj - blk
    ok = (jnp.abs(kj - blk - qi) <= blk) & (kpos >= 0) & (kpos < seq_len)
    bias = jnp.where(ok, 0.0, NEG_BIG).astype(F32)
    top = lax.broadcasted_iota(jnp.int32, (2 * blk, 1), 0) < blk
    lo_out = lax.broadcasted_iota(jnp.int32, (2 * blk, LANES), 1) < HEAD_DIM

    def softmax_parts(s, sink_col):
        s = s + bias
        m = jnp.maximum(jnp.max(s, axis=-1, keepdims=True), sink_col)
        p = jnp.exp(s - m)
        den = jnp.sum(p, axis=-1, keepdims=True) + jnp.exp(sink_col - m)
        return p.astype(BF16), 1.0 / den

    n_slab = ATT_HEADS * HEAD_DIM // LANES
    per_kv = n_slab // ATT_KV_HEADS
    for g in range(ATT_KV_HEADS):
        slabs = [q_ref[:, (per_kv * g + t) * LANES:(per_kv * g + t + 1) * LANES] for t in range(per_kv)]
        qs = jnp.concatenate(slabs, axis=0)
        h0 = 2 * per_kv * g
        sink_e = jnp.where(top, sink_ref[h0], sink_ref[h0 + 2])
        sink_o = jnp.where(top, sink_ref[h0 + 1], sink_ref[h0 + 3])
        p_e, r_e = softmax_parts(_dot_tb(qs, k_even[g]), sink_e)
        p_o, r_o = softmax_parts(_dot_tb(qs, k_odd[g]), sink_o)
        pv = _dot(jnp.concatenate([p_e, p_o], axis=1), jnp.concatenate([v_even[g], v_odd[g]], axis=0))
        o = pv * jnp.where(lo_out, r_e, r_o)
        for t in range(per_kv):
            o_ref[:, (per_kv * g + t) * LANES:(per_kv * g + t + 1) * LANES] = o[t * blk:(t + 1) * blk].astype(BF16)


def _attn(q, k2, v2, sink, batch, seq_len):
    blk = ATT_BLOCK
    nb = seq_len // blk
    cq = q.shape[-1]
    ck = k2.shape[-1]
    q3 = q.reshape(batch, seq_len, cq)
    k3 = k2.reshape(batch, seq_len, ck)
    v3 = v2.reshape(batch, seq_len, ck)
    prev = pl.BlockSpec((None, blk, ck), lambda b, i: (b, jnp.maximum(i - 1, 0), 0))
    cur = pl.BlockSpec((None, blk, ck), lambda b, i: (b, i, 0))
    nxt = pl.BlockSpec((None, blk, ck), lambda b, i: (b, jnp.minimum(i + 1, nb - 1), 0))
    qspec = pl.BlockSpec((None, blk, cq), lambda b, i: (b, i, 0))
    out = pl.pallas_call(
        functools.partial(_attn_kernel, seq_len=seq_len),
        grid=(batch, nb),
        in_specs=[pl.BlockSpec(memory_space=pltpu.SMEM), qspec, prev, cur, nxt, prev, cur, nxt],
        out_specs=qspec,
        out_shape=jax.ShapeDtypeStruct((batch, seq_len, cq), BF16),
        compiler_params=pltpu.CompilerParams(dimension_semantics=("parallel", "parallel"),
                                             vmem_limit_bytes=VMEM_LIMIT_BYTES),
        name="attn",
    )(sink, q3, k3, k3, k3, v3, v3, v3)
    return out.reshape(batch * seq_len, cq)


def _hgrn_coeffs(reverse):
    c, sub = HG_CHUNK, HG_SUB
    t = np.arange(c)[:, None]
    u = np.arange(c)[None, :]
    sc_t, sc_u = t // sub, u // sub
    n_sub = c // sub
    pi_t = (n_sub - 1 - sc_t) if reverse else sc_t
    pi_u = (n_sub - 1 - sc_u) if reverse else sc_u
    same = sc_t == sc_u
    upto = (u >= t) if reverse else (u <= t)
    m_rel = same & upto
    m_tot = same
    m_out = pi_u < pi_t
    m_in = pi_u > pi_t
    coeff = np.concatenate([m_rel, m_tot, m_out, m_in], axis=0).astype(np.float32)
    diag_mask = (same & upto).astype(np.float32)
    return coeff, diag_mask


def _hgrn_chunk(q, fpre, v, lb, coeff, diag_mask, st_ref, reverse):
    c, sub = HG_CHUNK, HG_SUB
    n_sub = c // sub
    f = lb + (1.0 - lb) * _sigmoid(fpre)
    g = jnp.log(f)
    kk = 1.0 - f
    sums = jnp.dot(coeff, g, preferred_element_type=F32, precision=lax.Precision.HIGHEST)
    b_rel, tot, p_out, p_in = sums[:c], sums[c:2 * c], sums[2 * c:3 * c], sums[3 * c:]
    q_rel = q * jnp.exp(b_rel)
    k_rest = kk * jnp.exp(tot - b_rel)
    k_diag = kk * jnp.exp(-b_rel)
    q_in = q_rel * jnp.exp(p_out)
    k_end = (k_rest * jnp.exp(p_in)).astype(BF16)
    gamma = jnp.exp(p_out[0:1] + tot[0:1] + p_in[0:1])

    order = list(range(n_sub))
    if reverse:
        order = order[::-1]
    rows = lambda s: slice(s * sub, (s + 1) * sub)
    def between(a0, a1):
        fac = None
        for m in range(a0 + 1, a1):
            e = jnp.exp(tot[rows(order[m])])
            fac = e if fac is None else fac * e
        return fac

    zeros_sub = jnp.zeros((sub, q.shape[1]), F32)
    lhs_blocks, rhs_blocks = [], []
    for a1 in range(1, n_sub):
        lhs = [zeros_sub] * n_sub
        lhs[order[a1]] = q_rel[rows(order[a1])]
        rhs = [zeros_sub] * n_sub
        for a0 in range(a1):
            kr = k_rest[rows(order[a0])]
            fac = between(a0, a1)
            rhs[order[a0]] = kr if fac is None else kr * fac
        lhs_blocks.append(jnp.concatenate(lhs, axis=0).astype(BF16))
        rhs_blocks.append(jnp.concatenate(rhs, axis=0).astype(BF16))
    q_rel_b = q_rel.astype(BF16)
    k_diag_b = k_diag.astype(BF16)
    q_in_b = q_in.astype(BF16)

    outs = []
    for h in range(HG_HEADS):
        hs = slice(h * HG_DIM, (h + 1) * HG_DIM)
        a = jnp.where(diag_mask > 0, _dot_tb(q_rel_b[:, hs], k_diag_b[:, hs]), 0.0)
        for lhs, rhs in zip(lhs_blocks, rhs_blocks):
            a = a + _dot_tb(lhs[:, hs], rhs[:, hs])
        st = st_ref[h]
        o_h = _dot(a.astype(BF16), v[:, hs]) + _dot_tb(q_in_b[:, hs], st.astype(BF16))
        st_ref[h] = st * gamma[:, hs] + _dot_ta(v[:, hs], k_end[:, hs])
        outs.append(o_h)
    return jnp.concatenate(outs, axis=1)


def _hgrn_fwd_kernel(q_ref, f_ref, v_ref, lb_ref, coeff_ref, mask_ref, o_ref, st_ref):
    @pl.when(pl.program_id(1) == 0)
    def _():
        st_ref[...] = jnp.zeros_like(st_ref)

    lb = lb_ref[...]
    coeff = coeff_ref[...]
    mask = mask_ref[...]

    def body(ci, carry):
        r = pl.ds(pl.multiple_of(ci * HG_CHUNK, HG_CHUNK), HG_CHUNK)
        o_ref[r, :] = _hgrn_chunk(q_ref[r, :], f_ref[r, :], v_ref[r, :], lb, coeff, mask, st_ref, False)
        return carry

    lax.fori_loop(0, HG_BLOCK // HG_CHUNK, body, 0)


def _hgrn_bwd_kernel(q_ref, f_ref, v_ref, lb_ref, coeff_ref, mask_ref, of_ref, gate_ref, ng_ref, o_ref, st_ref):
    @pl.when(pl.program_id(1) == 0)
    def _():
        st_ref[...] = jnp.zeros_like(st_ref)

    lb = lb_ref[...]
    coeff = coeff_ref[...]
    mask = mask_ref[...]
    ng = ng_ref[...]
    n_chunks = HG_BLOCK // HG_CHUNK

    def body(cj, carry):
        ci = n_chunks - 1 - cj
        r = pl.ds(pl.multiple_of(ci * HG_CHUNK, HG_CHUNK), HG_CHUNK)
        o = of_ref[r, :] + _hgrn_chunk(q_ref[r, :], f_ref[r, :], v_ref[r, :], lb, coeff, mask, st_ref, True)
        outs = []
        for h in range(HG_HEADS):
            hs = slice(h * HG_DIM, (h + 1) * HG_DIM)
            oh = o[:, hs]
            ms = jnp.mean(oh * oh, axis=-1, keepdims=True)
            outs.append(oh * lax.rsqrt(ms + RMS_EPS))
        on = jnp.concatenate(outs, axis=1) * ng
        o_ref[r, :] = (on * gate_ref[r, :].astype(F32)).astype(BF16)
        return carry

    lax.fori_loop(0, n_chunks, body, 0)


def _hgrn(hq, hf_f, hf_b, hi, hg_gate, lower, norm_g, batch, seq_len):
    hk = hq.shape[-1]
    tb = HG_BLOCK
    nblk = seq_len // tb
    r3 = lambda a: a.reshape(batch, seq_len, a.shape[-1])
    coeff_f, mask_f = _hgrn_coeffs(False)
    coeff_b, mask_b = _hgrn_coeffs(True)
    const = lambda shape: pl.BlockSpec(shape, lambda b, j: (0,) * len(shape))
    fwd_blk = pl.BlockSpec((None, tb, hk), lambda b, j: (b, j, 0))
    bwd_blk = pl.BlockSpec((None, tb, hk), lambda b, j: (b, nblk - 1 - j, 0))
    cparams = pltpu.CompilerParams(dimension_semantics=("parallel", "arbitrary"),
                                   vmem_limit_bytes=VMEM_LIMIT_BYTES)
    state = pltpu.VMEM((HG_HEADS, HG_DIM, HG_DIM), F32)
    o_f = pl.pallas_call(
        _hgrn_fwd_kernel,
        grid=(batch, nblk),
        in_specs=[fwd_blk, fwd_blk, fwd_blk, const((1, hk)), const(coeff_f.shape), const(mask_f.shape)],
        out_specs=fwd_blk,
        out_shape=jax.ShapeDtypeStruct((batch, seq_len, hk), F32),
        scratch_shapes=[state],
        compiler_params=cparams,
        name="hgrn_fwd",
    )(r3(hq), r3(hf_f), r3(hi), lower[0:1], jnp.asarray(coeff_f), jnp.asarray(mask_f))
    ng = jnp.tile(norm_g.astype(F32), HG_HEADS)[None, :]
    o = pl.pallas_call(
        _hgrn_bwd_kernel,
        grid=(batch, nblk),
        in_specs=[bwd_blk, bwd_blk, bwd_blk, const((1, hk)), const(coeff_b.shape), const(mask_b.shape),
                  bwd_blk, bwd_blk, const((1, hk))],
        out_specs=bwd_blk,
        out_shape=jax.ShapeDtypeStruct((batch, seq_len, hk), BF16),
        scratch_shapes=[state],
        compiler_params=cparams,
        name="hgrn_bwd",
    )(r3(hq), r3(hf_b), r3(hi), lower[1:2], jnp.asarray(coeff_b), jnp.asarray(mask_b), o_f, r3(hg_gate), ng)
    return o.reshape(batch * seq_len, hk)


def _layer_norm(y, g, b):
    mu = jnp.mean(y, axis=-1, keepdims=True)
    d = y - mu
    var = jnp.mean(d * d, axis=-1, keepdims=True)
    return d * lax.rsqrt(var + LN_EPS) * g + b


def _mix_ffn_kernel(x_ref, oa_ref, oh_ref, ga_ref, gb_ref, wpa_ref, wph_ref, wout_ref, w1_ref, w2_ref,
                    g1_ref, b1_ref, g2_ref, b2_ref, y_ref):
    mixed = (ga_ref[...].astype(F32) * _dot(oa_ref[...], wpa_ref[...])
             + gb_ref[...].astype(F32) * _dot(oh_ref[...], wph_ref[...]))
    x1 = _layer_norm(ALPHA * x_ref[...] + _dot(mixed.astype(BF16), wout_ref[...]), g1_ref[...], b1_ref[...])
    h = jnp.maximum(_dot(x1.astype(BF16), w1_ref[...]), 0.0)
    ff = _dot((h * h).astype(BF16), w2_ref[...])
    y_ref[...] = _layer_norm(ALPHA * x1 + ff, g2_ref[...], b2_ref[...])


def _mix_ffn(x2d, o_att, o_hg, ga, gb, wpa, wph, wout, w1, w2, g1, b1, g2, b2):
    n_tok, d_model = x2d.shape
    tm = FFN_ROWS
    row = lambda c: pl.BlockSpec((tm, c), lambda i: (i, 0))
    resident = lambda a: pl.BlockSpec(a.shape, lambda i: (0, 0), pipeline_mode=pl.Buffered(1))
    vec = lambda a: pl.BlockSpec((1, a.shape[-1]), lambda i: (0, 0))
    v2 = lambda a: a.reshape(1, -1).astype(F32)
    return pl.pallas_call(
        _mix_ffn_kernel,
        grid=(n_tok // tm,),
        in_specs=[row(d_model), row(o_att.shape[1]), row(o_hg.shape[1]), row(d_model), row(d_model),
                  resident(wpa), resident(wph), resident(wout), resident(w1), resident(w2),
                  vec(g1), vec(b1), vec(g2), vec(b2)],
        out_specs=row(d_model),
        out_shape=jax.ShapeDtypeStruct((n_tok, d_model), F32),
        compiler_params=pltpu.CompilerParams(dimension_semantics=("parallel",),
                                             vmem_limit_bytes=VMEM_LIMIT_BYTES),
        name="mix_ffn",
    )(x2d, o_att, o_hg, ga, gb, wpa, wph, wout, w1, w2, v2(g1), v2(b1), v2(g2), v2(b2))


def _rope_tables(seq_len):
    half = ROPE_DIM // 2
    inv = ROPE_THETA ** (-jnp.arange(0, ROPE_DIM, 2, dtype=F32) / ROPE_DIM)
    ang = jnp.arange(seq_len, dtype=F32)[:, None] * inv[None, :]
    cos, sin = jnp.cos(ang), jnp.sin(ang)
    ones = jnp.ones((seq_len, HEAD_DIM - ROPE_DIM), F32)
    zeros_rest = jnp.zeros((seq_len, HEAD_DIM - ROPE_DIM), F32)
    zeros_half = jnp.zeros((seq_len, half), F32)
    c = jnp.concatenate([cos, cos, ones], axis=1)
    s1 = jnp.concatenate([zeros_half, sin, zeros_rest], axis=1)
    s2 = jnp.concatenate([-sin, zeros_half, zeros_rest], axis=1)
    rep = LANES // HEAD_DIM
    return tuple(jnp.tile(t, (1, rep)) for t in (c, s1, s2))


def _trunk(x, params, lower, rope_tabs):
    batch, seq_len, d_model = x.shape
    x2d = x.reshape(batch * seq_len, d_model)
    for l in range(DEPTH):
        p = params[l]
        q, k2, v2, hq, hf_f, hf_b, hi, hg_gate, ga, gb = _proj(x2d, p["w_in"], rope_tabs, seq_len, d_model)
        o_att = _attn(q, k2, v2, p["sink"], batch, seq_len)
        o_hg = _hgrn(hq, hf_f, hf_b, hi, hg_gate, lower[l], p["norm_g"], batch, seq_len)
        x2d = _mix_ffn(x2d, o_att, o_hg, ga, gb, p["wpa"], p["wph"], p["wout"], p["w1"], p["w2"],
                       p["g1"], p["b1"], p["g2"], p["b2"])
    return x2d.reshape(batch, seq_len, d_model)


def kernel(x_prompt, x_sample, w_in, att_sink, hgrn_lb, hgrn_norm_g, w_proj_att, w_proj_hgrn, w_out,
           ln1_g, ln1_b, w_ff1, w_ff2, ln2_g, ln2_b):
    sm = jax.nn.softmax(hgrn_lb.astype(F32), axis=0)
    lower = jnp.cumsum(sm, axis=0) - sm[0:1]
    params = []
    for l in range(DEPTH):
        params.append(dict(
            w_in=w_in[l].astype(BF16), sink=att_sink[l].astype(F32), norm_g=hgrn_norm_g[l],
            wpa=w_proj_att[l].astype(BF16), wph=w_proj_hgrn[l].astype(BF16), wout=w_out[l].astype(BF16),
            w1=w_ff1[l].astype(BF16), w2=w_ff2[l].astype(BF16),
            g1=ln1_g[l], b1=ln1_b[l], g2=ln2_g[l], b2=ln2_b[l]))
    rope_tabs = _rope_tables(x_prompt.shape[1])
    y_prompt = _trunk(x_prompt, params, lower, rope_tabs)
    if x_sample.shape[1] != x_prompt.shape[1]:
        rope_tabs = _rope_tables(x_sample.shape[1])
    y_sample = _trunk(x_sample, params, lower, rope_tabs)
    return (y_prompt, y_sample)
```

```python
import functools

import jax
import jax.numpy as jnp
import numpy as np
from jax import lax
from jax.experimental import pallas as pl
from jax.experimental.pallas import tpu as pltpu

F32 = jnp.float32
BF16 = jnp.bfloat16

ATT_HEADS = 8
ATT_KV_HEADS = 2
HEAD_DIM = 64
ATT_BLOCK = 128
ROPE_THETA = 500000.0
ROPE_DIM = HEAD_DIM // 4
HG_HEADS = 8
HG_DIM = 64
DEPTH = 2
ALPHA = (2 * DEPTH) ** 0.25
LN_EPS = 1e-5
RMS_EPS = 1e-6
NEG_BIG = -1e30

LANES = 128
MXU_DIM = 256
VMEM_LIMIT_BYTES = 56 * 1024 * 1024

PROJ_ROWS = 512
FFN_ROWS = 256
HG_BLOCK = 256
HG_CHUNK = 64
HG_SUB = 16
HG_SLAB = MXU_DIM
HG_SLAB_HEADS = HG_SLAB // HG_DIM


def _sigmoid(x):
    return 1.0 / (1.0 + jnp.exp(-x))


def _dot(a, b):
    return jnp.dot(a, b, preferred_element_type=F32)


def _dot_tb(a, b):
    return lax.dot_general(a, b, (((1,), (1,)), ((), ())), preferred_element_type=F32)


def _dot_ta(a, b):
    return lax.dot_general(a, b, (((0,), (0,)), ((), ())), preferred_element_type=F32)


def _proj_kernel(x_ref, w_ref, rc_ref, rs1_ref, rs2_ref,
                 q_ref, k_ref, v_ref, hq_ref, ff_ref, fb_ref, hi_ref, hg_ref, ga_ref, gb_ref,
                 *, cuts):
    xb = x_ref[...].astype(BF16)

    def mm(name):
        a, b = cuts[name]
        return _dot(xb, w_ref[:, a:b])

    rc = rc_ref[...]
    rs1 = rs1_ref[...]
    rs2 = rs2_ref[...]

    def rope(blk):
        return blk * rc + pltpu.roll(blk, ROPE_DIM // 2, 1) * rs1 + pltpu.roll(blk, LANES - ROPE_DIM // 2, 1) * rs2

    aq = mm("aq")
    for j in range(aq.shape[1] // LANES):
        blk = rope(aq[:, j * LANES:(j + 1) * LANES]) * (HEAD_DIM ** -0.5)
        q_ref[:, j * LANES:(j + 1) * LANES] = blk.astype(BF16)
    ak = rope(mm("ak"))
    k_ref[:, :LANES] = ak.astype(BF16)
    k_ref[:, LANES:] = pltpu.roll(ak, HEAD_DIM, 1).astype(BF16)
    av = mm("av")
    v_ref[:, :LANES] = av.astype(BF16)
    v_ref[:, LANES:] = pltpu.roll(av, HEAD_DIM, 1).astype(BF16)

    hq = mm("hq")
    hq_ref[...] = hq * _sigmoid(hq) * (HG_DIM ** -0.5)
    ff_ref[...] = mm("hf_f")
    fb_ref[...] = mm("hf_b")
    hi_ref[...] = mm("hi").astype(BF16)
    hg = mm("hg")
    hg_ref[...] = (hg * _sigmoid(hg)).astype(BF16)
    ga_ref[...] = _sigmoid(mm("ga")).astype(BF16)
    gb_ref[...] = _sigmoid(mm("gb")).astype(BF16)


def _proj(x2d, w_in, rope_tabs, seq_len, d_model):
    n_tok = x2d.shape[0]
    tm = PROJ_ROWS
    att_q = ATT_HEADS * HEAD_DIM
    att_kv = ATT_KV_HEADS * HEAD_DIM
    hg = HG_HEADS * HG_DIM
    sizes = [("aq", att_q), ("ak", att_kv), ("av", att_kv), ("hq", hg), ("hf_f", hg), ("hf_b", hg),
             ("hi", hg), ("hg", hg), ("ga", d_model), ("gb", d_model)]
    cuts, off = {}, 0
    for name, s in sizes:
        cuts[name] = (off, off + s)
        off += s
    d_in = off
    pos_blocks = seq_len // tm
    row = lambda c: pl.BlockSpec((tm, c), lambda i: (i, 0))
    tab = pl.BlockSpec((tm, LANES), lambda i: (i % pos_blocks, 0))
    out_shapes = [
        jax.ShapeDtypeStruct((n_tok, att_q), BF16),
        jax.ShapeDtypeStruct((n_tok, 2 * att_kv), BF16),
        jax.ShapeDtypeStruct((n_tok, 2 * att_kv), BF16),
        jax.ShapeDtypeStruct((n_tok, hg), F32),
        jax.ShapeDtypeStruct((n_tok, hg), F32),
        jax.ShapeDtypeStruct((n_tok, hg), F32),
        jax.ShapeDtypeStruct((n_tok, hg), BF16),
        jax.ShapeDtypeStruct((n_tok, hg), BF16),
        jax.ShapeDtypeStruct((n_tok, d_model), BF16),
        jax.ShapeDtypeStruct((n_tok, d_model), BF16),
    ]
    return pl.pallas_call(
        functools.partial(_proj_kernel, cuts=cuts),
        grid=(n_tok // tm,),
        in_specs=[row(d_model),
                  pl.BlockSpec((d_model, d_in), lambda i: (0, 0), pipeline_mode=pl.Buffered(1)),
                  tab, tab, tab],
        out_specs=[row(s.shape[1]) for s in out_shapes],
        out_shape=out_shapes,
        compiler_params=pltpu.CompilerParams(dimension_semantics=("parallel",),
                                             vmem_limit_bytes=VMEM_LIMIT_BYTES),
        name="proj",
    )(x2d, w_in, *rope_tabs)


def _attn_kernel(sink_ref, q_ref, kp_ref, kc_ref, kn_ref, vp_ref, vc_ref, vn_ref, o_ref, *, seq_len):
    blk = ATT_BLOCK
    win = 3 * blk
    i = pl.program_id(1)
    k = jnp.concatenate([kp_ref[...], kc_ref[...], kn_ref[...]], axis=0)
    v = jnp.concatenate([vp_ref[...], vc_ref[...], vn_ref[...]], axis=0)
    lo = lax.broadcasted_iota(jnp.int32, (win, LANES), 1) < HEAD_DIM
    zero = jnp.zeros((win, LANES), BF16)
    k_arr, k_rot = k[:, :LANES], k[:, LANES:]
    v_arr, v_rot = v[:, :LANES], v[:, LANES:]
    k_even = [jnp.where(lo, k_arr, zero), jnp.where(lo, k_rot, zero)]
    k_odd = [jnp.where(lo, zero, k_rot), jnp.where(lo, zero, k_arr)]
    v_even = [jnp.where(lo, v_arr, zero), jnp.where(lo, v_rot, zero)]
    v_odd = [jnp.where(lo, zero, v_rot), jnp.where(lo, zero, v_arr)]

    qi = lax.broadcasted_iota(jnp.int32, (2 * blk, win), 0) % blk
    kj = lax.broadcasted_iota(jnp.int32, (2 * blk, win), 1)
    kpos = i * blk + kj - blk
    ok = (jnp.abs(kj - blk - qi) <= blk) & (kpos >= 0) & (kpos < seq_len)
    bias = jnp.where(ok, 0.0, NEG_BIG).astype(F32)
    top = lax.broadcasted_iota(jnp.int32, (2 * blk, 1), 0) < blk
    lo_out = lax.broadcasted_iota(jnp.int32, (2 * blk, LANES), 1) < HEAD_DIM

    def softmax_parts(s, sink_col):
        s = s + bias
        m = jnp.maximum(jnp.max(s, axis=-1, keepdims=True), sink_col)
        p = jnp.exp(s - m)
        den = jnp.sum(p, axis=-1, keepdims=True) + jnp.exp(sink_col - m)
        return p.astype(BF16), 1.0 / den

    n_slab = ATT_HEADS * HEAD_DIM // LANES
    per_kv = n_slab // ATT_KV_HEADS
    for g in range(ATT_KV_HEADS):
        slabs = [q_ref[:, (per_kv * g + t) * LANES:(per_kv * g + t + 1) * LANES] for t in range(per_kv)]
        qs = jnp.concatenate(slabs, axis=0)
        h0 = 2 * per_kv * g
        sink_e = jnp.where(top, sink_ref[h0], sink_ref[h0 + 2])
        sink_o = jnp.where(top, sink_ref[h0 + 1], sink_ref[h0 + 3])
        p_e, r_e = softmax_parts(_dot_tb(qs, k_even[g]), sink_e)
        p_o, r_o = softmax_parts(_dot_tb(qs, k_odd[g]), sink_o)
        pv = _dot(jnp.concatenate([p_e, p_o], axis=1), jnp.concatenate([v_even[g], v_odd[g]], axis=0))
        o = pv * jnp.where(lo_out, r_e, r_o)
        for t in range(per_kv):
            o_ref[:, (per_kv * g + t) * LANES:(per_kv * g + t + 1) * LANES] = o[t * blk:(t + 1) * blk].astype(BF16)


def _attn(q, k2, v2, sink, batch, seq_len):
    blk = ATT_BLOCK
    nb = seq_len // blk
    cq = q.shape[-1]
    ck = k2.shape[-1]
    q3 = q.reshape(batch, seq_len, cq)
    k3 = k2.reshape(batch, seq_len, ck)
    v3 = v2.reshape(batch, seq_len, ck)
    prev = pl.BlockSpec((None, blk, ck), lambda b, i: (b, jnp.maximum(i - 1, 0), 0))
    cur = pl.BlockSpec((None, blk, ck), lambda b, i: (b, i, 0))
    nxt = pl.BlockSpec((None, blk, ck), lambda b, i: (b, jnp.minimum(i + 1, nb - 1), 0))
    qspec = pl.BlockSpec((None, blk, cq), lambda b, i: (b, i, 0))
    out = pl.pallas_call(
        functools.partial(_attn_kernel, seq_len=seq_len),
        grid=(batch, nb),
        in_specs=[pl.BlockSpec(memory_space=pltpu.SMEM), qspec, prev, cur, nxt, prev, cur, nxt],
        out_specs=qspec,
        out_shape=jax.ShapeDtypeStruct((batch, seq_len, cq), BF16),
        compiler_params=pltpu.CompilerParams(dimension_semantics=("parallel", "parallel"),
                                             vmem_limit_bytes=VMEM_LIMIT_BYTES),
        name="attn",
    )(sink, q3, k3, k3, k3, v3, v3, v3)
    return out.reshape(batch * seq_len, cq)


def _hgrn_consts(reverse):
    c, sub = HG_CHUNK, HG_SUB
    n_sub = c // sub
    t = np.arange(c)[:, None]
    s = np.arange(c)[None, :]
    same = (t // sub) == (s // sub)
    upto = (s >= t) if reverse else (s <= t)
    coeff = np.concatenate([same & upto, same], axis=0).astype(np.float32)
    order = list(range(n_sub))[::-1] if reverse else list(range(n_sub))
    cols = [np.broadcast_to((t // sub) == order[a], (c, c)) for a in range(1, n_sub)]
    cols.append(same & upto)
    rmask = np.tile(np.concatenate(cols, axis=1), (HG_SLAB_HEADS, 1)).astype(np.float32)
    lane_head = np.arange(HG_SLAB)[None, :] // HG_DIM
    row_head = np.arange(HG_SLAB)[:, None] // HG_DIM
    bdmask = (lane_head == row_head).astype(np.float32)
    return coeff, rmask, bdmask


def _hgrn_block(q, fpre, v, lb, coeff, rmask, bdmask, st_ref, reverse):
    c, sub = HG_CHUNK, HG_SUB
    n_sub = c // sub
    n_chunks = q.shape[0] // c
    hk = q.shape[1]
    n_slab = hk // HG_SLAB
    f = lb + (1.0 - lb) * _sigmoid(fpre)
    g = jnp.log(f)
    kk = 1.0 - f
    g_hi = g.astype(BF16)
    g_lo = (g - g_hi.astype(F32)).astype(BF16)
    order = list(range(n_sub))[::-1] if reverse else list(range(n_sub))
    rows = lambda m: slice(m * sub, (m + 1) * sub)
    lane_head = (lax.broadcasted_iota(jnp.int32, (c, HG_SLAB), 1) // HG_DIM)
    head_sel = [lane_head == h for h in range(HG_SLAB_HEADS)]
    keep_r = rmask > 0
    keep_bd = bdmask > 0
    zeros_sub = jnp.zeros((sub, hk), F32)
    st = [st_ref[j] for j in range(n_slab)]
    outs = [None] * n_chunks
    for step in range(n_chunks):
        ci = n_chunks - 1 - step if reverse else step
        r = slice(ci * c, (ci + 1) * c)
        sums = _dot(coeff, g_hi[r]) + _dot(coeff, g_lo[r])
        b_rel, tot = sums[:c], sums[c:]
        q_rel = q[r] * jnp.exp(b_rel)
        k_rest = kk[r] * jnp.exp(tot - b_rel)
        k_diag = kk[r] * jnp.exp(-b_rel)
        t_sub = [tot[rows(order[a])] for a in range(n_sub)]
        e_sub = [jnp.exp(t) for t in t_sub]
        p_out, acc = [None] * n_sub, None
        for a in range(n_sub):
            p_out[a] = acc
            acc = e_sub[a] if acc is None else acc * e_sub[a]
        gamma = acc[0:1]
        p_in, acc = [None] * n_sub, None
        for a in reversed(range(n_sub)):
            p_in[a] = acc
            acc = e_sub[a] if acc is None else acc * e_sub[a]
        q_in, k_end = [None] * n_sub, [None] * n_sub
        for a in range(n_sub):
            m = order[a]
            q_in[m] = q_rel[rows(m)] if p_out[a] is None else q_rel[rows(m)] * p_out[a]
            k_end[m] = k_rest[rows(m)] if p_in[a] is None else k_rest[rows(m)] * p_in[a]
        q_in = jnp.concatenate(q_in, axis=0).astype(BF16)
        k_end = jnp.concatenate(k_end, axis=0).astype(BF16)
        variants = []
        for a1 in range(1, n_sub):
            parts, fac = [zeros_sub] * n_sub, None
            for a0 in range(a1 - 1, -1, -1):
                kr = k_rest[rows(order[a0])]
                parts[order[a0]] = kr if fac is None else kr * fac
                fac = e_sub[a0] if fac is None else fac * e_sub[a0]
            variants.append(jnp.concatenate(parts, axis=0))
        variants.append(k_diag)
        k_stack = jnp.concatenate(variants, axis=0).astype(BF16)
        v_c = v[r]
        o_slabs = []
        for j in range(n_slab):
            ls = slice(j * HG_SLAB, (j + 1) * HG_SLAB)
            qs = q_rel[:, ls]
            lhs = jnp.concatenate([jnp.where(head_sel[h], qs, 0.0) for h in range(HG_SLAB_HEADS)],
                                  axis=0).astype(BF16)
            score = jnp.where(keep_r, _dot_tb(lhs, k_stack[:, ls]), 0.0).astype(BF16)
            vs = v_c[:, ls]
            o_all = _dot(score, jnp.concatenate([vs] * n_sub, axis=0))
            o_intra = None
            for h in range(HG_SLAB_HEADS):
                part = jnp.where(head_sel[h], o_all[h * c:(h + 1) * c], 0.0)
                o_intra = part if o_intra is None else o_intra + part
            o_inter = _dot_tb(q_in[:, ls], st[j].astype(BF16))
            st[j] = st[j] * gamma[:, ls] + jnp.where(keep_bd, _dot_ta(vs, k_end[:, ls]), 0.0)
            o_slabs.append(o_intra + o_inter)
        outs[ci] = jnp.concatenate(o_slabs, axis=1)
    for j in range(n_slab):
        st_ref[j] = st[j]
    return jnp.concatenate(outs, axis=0)


def _hgrn_fwd_kernel(q_ref, f_ref, v_ref, lb_ref, coeff_ref, rmask_ref, bdmask_ref, o_ref, st_ref):
    @pl.when(pl.program_id(1) == 0)
    def _():
        st_ref[...] = jnp.zeros_like(st_ref)

    o_ref[...] = _hgrn_block(q_ref[...], f_ref[...], v_ref[...], lb_ref[...], coeff_ref[...],
                             rmask_ref[...], bdmask_ref[...], st_ref, False)


def _hgrn_bwd_kernel(q_ref, f_ref, v_ref, lb_ref, coeff_ref, rmask_ref, bdmask_ref, of_ref, gate_ref, ng_ref,
                     o_ref, st_ref):
    @pl.when(pl.program_id(1) == 0)
    def _():
        st_ref[...] = jnp.zeros_like(st_ref)

    bdmask = bdmask_ref[...]
    o = of_ref[...] + _hgrn_block(q_ref[...], f_ref[...], v_ref[...], lb_ref[...], coeff_ref[...],
                                  rmask_ref[...], bdmask, st_ref, True)
    o2 = (o * o).astype(BF16)
    ones_bd = bdmask.astype(BF16)
    ms = jnp.concatenate([_dot(o2[:, j * HG_SLAB:(j + 1) * HG_SLAB], ones_bd)
                          for j in range(o.shape[1] // HG_SLAB)], axis=1) * (1.0 / HG_DIM)
    on = o * lax.rsqrt(ms + RMS_EPS) * ng_ref[...]
    o_ref[...] = (on * gate_ref[...].astype(F32)).astype(BF16)


def _hgrn(hq, hf_f, hf_b, hi, hg_gate, lower, norm_g, batch, seq_len):
    hk = hq.shape[-1]
    tb = HG_BLOCK
    nblk = seq_len // tb
    r3 = lambda a: a.reshape(batch, seq_len, a.shape[-1])
    const = lambda a: pl.BlockSpec(a.shape, lambda b, j: (0,) * a.ndim)
    fwd_blk = pl.BlockSpec((None, tb, hk), lambda b, j: (b, j, 0))
    bwd_blk = pl.BlockSpec((None, tb, hk), lambda b, j: (b, nblk - 1 - j, 0))
    cparams = pltpu.CompilerParams(dimension_semantics=("parallel", "arbitrary"),
                                   vmem_limit_bytes=VMEM_LIMIT_BYTES)
    state = pltpu.VMEM((hk // HG_SLAB, HG_SLAB, HG_SLAB), F32)
    lb_f, lb_b = lower[0:1], lower[1:2]
    consts_f = [jnp.asarray(a, dt) for a, dt in zip(_hgrn_consts(False), (BF16, F32, F32))]
    consts_b = [jnp.asarray(a, dt) for a, dt in zip(_hgrn_consts(True), (BF16, F32, F32))]
    o_f = pl.pallas_call(
        _hgrn_fwd_kernel,
        grid=(batch, nblk),
        in_specs=[fwd_blk, fwd_blk, fwd_blk, const(lb_f)] + [const(a) for a in consts_f],
        out_specs=fwd_blk,
        out_shape=jax.ShapeDtypeStruct((batch, seq_len, hk), F32),
        scratch_shapes=[state],
        compiler_params=cparams,
        name="hgrn_fwd",
    )(r3(hq), r3(hf_f), r3(hi), lb_f, *consts_f)
    ng = jnp.tile(norm_g.astype(F32), HG_HEADS)[None, :]
    o = pl.pallas_call(
        _hgrn_bwd_kernel,
        grid=(batch, nblk),
        in_specs=[bwd_blk, bwd_blk, bwd_blk, const(lb_b)] + [const(a) for a in consts_b]
                 + [bwd_blk, bwd_blk, const(ng)],
        out_specs=bwd_blk,
        out_shape=jax.ShapeDtypeStruct((batch, seq_len, hk), BF16),
        scratch_shapes=[state],
        compiler_params=cparams,
        name="hgrn_bwd",
    )(r3(hq), r3(hf_b), r3(hi), lb_b, *consts_b, o_f, r3(hg_gate), ng)
    return o.reshape(batch * seq_len, hk)


def _layer_norm(y, g, b):
    mu = jnp.mean(y, axis=-1, keepdims=True)
    d = y - mu
    var = jnp.mean(d * d, axis=-1, keepdims=True)
    return d * lax.rsqrt(var + LN_EPS) * g + b


def _mix_ffn_kernel(x_ref, oa_ref, oh_ref, ga_ref, gb_ref, wpa_ref, wph_ref, wout_ref, w1_ref, w2_ref,
                    g1_ref, b1_ref, g2_ref, b2_ref, y_ref):
    mixed = (ga_ref[...].astype(F32) * _dot(oa_ref[...], wpa_ref[...])
             + gb_ref[...].astype(F32) * _dot(oh_ref[...], wph_ref[...]))
    x1 = _layer_norm(ALPHA * x_ref[...] + _dot(mixed.astype(BF16), wout_ref[...]), g1_ref[...], b1_ref[...])
    h = jnp.maximum(_dot(x1.astype(BF16), w1_ref[...]), 0.0)
    ff = _dot((h * h).astype(BF16), w2_ref[...])
    y_ref[...] = _layer_norm(ALPHA * x1 + ff, g2_ref[...], b2_ref[...])


def _mix_ffn(x2d, o_att, o_hg, ga, gb, wpa, wph, wout, w1, w2, g1, b1, g2, b2):
    n_tok, d_model = x2d.shape
    tm = FFN_ROWS
    row = lambda c: pl.BlockSpec((tm, c), lambda i: (i, 0))
    resident = lambda a: pl.BlockSpec(a.shape, lambda i: (0, 0), pipeline_mode=pl.Buffered(1))
    vec = lambda a: pl.BlockSpec((1, a.shape[-1]), lambda i: (0, 0))
    v2 = lambda a: a.reshape(1, -1).astype(F32)
    return pl.pallas_call(
        _mix_ffn_kernel,
        grid=(n_tok // tm,),
        in_specs=[row(d_model), row(o_att.shape[1]), row(o_hg.shape[1]), row(d_model), row(d_model),
                  resident(wpa), resident(wph), resident(wout), resident(w1), resident(w2),
                  vec(g1), vec(b1), vec(g2), vec(b2)],
        out_specs=row(d_model),
        out_shape=jax.ShapeDtypeStruct((n_tok, d_model), F32),
        compiler_params=pltpu.CompilerParams(dimension_semantics=("parallel",),
                                             vmem_limit_bytes=VMEM_LIMIT_BYTES),
        name="mix_ffn",
    )(x2d, o_att, o_hg, ga, gb, wpa, wph, wout, w1, w2, v2(g1), v2(b1), v2(g2), v2(b2))


def _rope_tables(seq_len):
    half = ROPE_DIM // 2
    inv = ROPE_THETA ** (-jnp.arange(0, ROPE_DIM, 2, dtype=F32) / ROPE_DIM)
    ang = jnp.arange(seq_len, dtype=F32)[:, None] * inv[None, :]
    cos, sin = jnp.cos(ang), jnp.sin(ang)
    ones = jnp.ones((seq_len, HEAD_DIM - ROPE_DIM), F32)
    zeros_rest = jnp.zeros((seq_len, HEAD_DIM - ROPE_DIM), F32)
    zeros_half = jnp.zeros((seq_len, half), F32)
    c = jnp.concatenate([cos, cos, ones], axis=1)
    s1 = jnp.concatenate([zeros_half, sin, zeros_rest], axis=1)
    s2 = jnp.concatenate([-sin, zeros_half, zeros_rest], axis=1)
    rep = LANES // HEAD_DIM
    return tuple(jnp.tile(t, (1, rep)) for t in (c, s1, s2))


def _trunk(x, params, lower, rope_tabs):
    batch, seq_len, d_model = x.shape
    x2d = x.reshape(batch * seq_len, d_model)
    for l in range(DEPTH):
        p = params[l]
        q, k2, v2, hq, hf_f, hf_b, hi, hg_gate, ga, gb = _proj(x2d, p["w_in"], rope_tabs, seq_len, d_model)
        o_att = _attn(q, k2, v2, p["sink"], batch, seq_len)
        o_hg = _hgrn(hq, hf_f, hf_b, hi, hg_gate, lower[l], p["norm_g"], batch, seq_len)
        x2d = _mix_ffn(x2d, o_att, o_hg, ga, gb, p["wpa"], p["wph"], p["wout"], p["w1"], p["w2"],
                       p["g1"], p["b1"], p["g2"], p["b2"])
    return x2d.reshape(batch, seq_len, d_model)


def kernel(x_prompt, x_sample, w_in, att_sink, hgrn_lb, hgrn_norm_g, w_proj_att, w_proj_hgrn, w_out,
           ln1_g, ln1_b, w_ff1, w_ff2, ln2_g, ln2_b):
    sm = jax.nn.softmax(hgrn_lb.astype(F32), axis=0)
    lower = jnp.cumsum(sm, axis=0) - sm[0:1]
    params = []
    for l in range(DEPTH):
        params.append(dict(
            w_in=w_in[l].astype(BF16), sink=att_sink[l].astype(F32), norm_g=hgrn_norm_g[l],
            wpa=w_proj_att[l].astype(BF16), wph=w_proj_hgrn[l].astype(BF16), wout=w_out[l].astype(BF16),
            w1=w_ff1[l].astype(BF16), w2=w_ff2[l].astype(BF16),
            g1=ln1_g[l], b1=ln1_b[l], g2=ln2_g[l], b2=ln2_b[l]))
    rope_tabs = _rope_tables(x_prompt.shape[1])
    y_prompt = _trunk(x_prompt, params, lower, rope_tabs)
    if x_sample.shape[1] != x_prompt.shape[1]:
        rope_tabs = _rope_tables(x_sample.shape[1])
    y_sample = _trunk(x_sample, params, lower, rope_tabs)
    return (y_prompt, y_sample)
```

```python
import functools

import jax
import jax.numpy as jnp
import numpy as np
from jax import lax
from jax.experimental import pallas as pl
from jax.experimental.pallas import tpu as pltpu

F32 = jnp.float32
BF16 = jnp.bfloat16

ATT_HEADS = 8
ATT_KV_HEADS = 2
HEAD_DIM = 64
ATT_BLOCK = 128
ROPE_THETA = 500000.0
ROPE_DIM = HEAD_DIM // 4
HG_HEADS = 8
HG_DIM = 64
DEPTH = 2
ALPHA = (2 * DEPTH) ** 0.25
LN_EPS = 1e-5
RMS_EPS = 1e-6
NEG_BIG = -1e30
LOG2E = 1.4426950408889634

LANES = 128
MXU_DIM = 256
VMEM_LIMIT_BYTES = 56 * 1024 * 1024

PROJ_ROWS = 512
FFN_ROWS = 256
HG_BLOCK = 512
HG_CHUNK = 64
HG_SUB = 16
HG_SLAB = MXU_DIM
HG_SLAB_HEADS = HG_SLAB // HG_DIM


def _sigmoid(x):
    return 1.0 / (1.0 + jnp.exp(-x))


def _dot(a, b):
    return jnp.dot(a, b, preferred_element_type=F32)


def _dot_tb(a, b):
    return lax.dot_general(a, b, (((1,), (1,)), ((), ())), preferred_element_type=F32)


def _dot_ta(a, b):
    return lax.dot_general(a, b, (((0,), (0,)), ((), ())), preferred_element_type=F32)


def _proj_kernel(x_ref, w_ref, rc_ref, rs1_ref, rs2_ref,
                 q_ref, k_ref, v_ref, hq_ref, ff_ref, fb_ref, hi_ref, hg_ref, ga_ref, gb_ref,
                 *, cuts):
    xb = x_ref[...].astype(BF16)

    def mm(name):
        a, b = cuts[name]
        return _dot(xb, w_ref[:, a:b])

    rc = rc_ref[...]
    rs1 = rs1_ref[...]
    rs2 = rs2_ref[...]

    def rope(blk):
        return blk * rc + pltpu.roll(blk, ROPE_DIM // 2, 1) * rs1 + pltpu.roll(blk, LANES - ROPE_DIM // 2, 1) * rs2

    aq = mm("aq")
    for j in range(aq.shape[1] // LANES):
        blk = rope(aq[:, j * LANES:(j + 1) * LANES]) * (HEAD_DIM ** -0.5)
        q_ref[:, j * LANES:(j + 1) * LANES] = blk.astype(BF16)
    ak = rope(mm("ak"))
    k_ref[:, :LANES] = ak.astype(BF16)
    k_ref[:, LANES:] = pltpu.roll(ak, HEAD_DIM, 1).astype(BF16)
    av = mm("av")
    v_ref[:, :LANES] = av.astype(BF16)
    v_ref[:, LANES:] = pltpu.roll(av, HEAD_DIM, 1).astype(BF16)

    hq = mm("hq")
    hq_ref[...] = hq * _sigmoid(hq) * (HG_DIM ** -0.5)
    ff_ref[...] = mm("hf_f")
    fb_ref[...] = mm("hf_b")
    hi_ref[...] = mm("hi").astype(BF16)
    hg = mm("hg")
    hg_ref[...] = (hg * _sigmoid(hg)).astype(BF16)
    ga_ref[...] = _sigmoid(mm("ga")).astype(BF16)
    gb_ref[...] = _sigmoid(mm("gb")).astype(BF16)


def _proj(x2d, w_in, rope_tabs, seq_len, d_model):
    n_tok = x2d.shape[0]
    tm = PROJ_ROWS
    att_q = ATT_HEADS * HEAD_DIM
    att_kv = ATT_KV_HEADS * HEAD_DIM
    hg = HG_HEADS * HG_DIM
    sizes = [("aq", att_q), ("ak", att_kv), ("av", att_kv), ("hq", hg), ("hf_f", hg), ("hf_b", hg),
             ("hi", hg), ("hg", hg), ("ga", d_model), ("gb", d_model)]
    cuts, off = {}, 0
    for name, s in sizes:
        cuts[name] = (off, off + s)
        off += s
    d_in = off
    pos_blocks = seq_len // tm
    row = lambda c: pl.BlockSpec((tm, c), lambda i: (i, 0))
    tab = pl.BlockSpec((tm, LANES), lambda i: (i % pos_blocks, 0))
    out_shapes = [
        jax.ShapeDtypeStruct((n_tok, att_q), BF16),
        jax.ShapeDtypeStruct((n_tok, 2 * att_kv), BF16),
        jax.ShapeDtypeStruct((n_tok, 2 * att_kv), BF16),
        jax.ShapeDtypeStruct((n_tok, hg), F32),
        jax.ShapeDtypeStruct((n_tok, hg), F32),
        jax.ShapeDtypeStruct((n_tok, hg), F32),
        jax.ShapeDtypeStruct((n_tok, hg), BF16),
        jax.ShapeDtypeStruct((n_tok, hg), BF16),
        jax.ShapeDtypeStruct((n_tok, d_model), BF16),
        jax.ShapeDtypeStruct((n_tok, d_model), BF16),
    ]
    return pl.pallas_call(
        functools.partial(_proj_kernel, cuts=cuts),
        grid=(n_tok // tm,),
        in_specs=[row(d_model),
                  pl.BlockSpec((d_model, d_in), lambda i: (0, 0), pipeline_mode=pl.Buffered(1)),
                  tab, tab, tab],
        out_specs=[row(s.shape[1]) for s in out_shapes],
        out_shape=out_shapes,
        compiler_params=pltpu.CompilerParams(dimension_semantics=("parallel",),
                                             vmem_limit_bytes=VMEM_LIMIT_BYTES),
        name="proj",
    )(x2d, w_in, *rope_tabs)


def _attn_kernel(sink_ref, q_ref, kp_ref, kc_ref, kn_ref, vp_ref, vc_ref, vn_ref, o_ref, *, seq_len):
    blk = ATT_BLOCK
    win = 3 * blk
    i = pl.program_id(1)
    k = jnp.concatenate([kp_ref[...], kc_ref[...], kn_ref[...]], axis=0)
    v = jnp.concatenate([vp_ref[...], vc_ref[...], vn_ref[...]], axis=0)
    lo = lax.broadcasted_iota(jnp.int32, (win, LANES), 1) < HEAD_DIM
    zero = jnp.zeros((win, LANES), BF16)
    k_arr, k_rot = k[:, :LANES], k[:, LANES:]
    v_arr, v_rot = v[:, :LANES], v[:, LANES:]
    k_even = [jnp.where(lo, k_arr, zero), jnp.where(lo, k_rot, zero)]
    k_odd = [jnp.where(lo, zero, k_rot), jnp.where(lo, zero, k_arr)]
    v_even = [jnp.where(lo, v_arr, zero), jnp.where(lo, v_rot, zero)]
    v_odd = [jnp.where(lo, zero, v_rot), jnp.where(lo, zero, v_arr)]

    qi = lax.broadcasted_iota(jnp.int32, (2 * blk, win), 0) % blk
    kj = lax.broadcasted_iota(jnp.int32, (2 * blk, win), 1)
    kpos = i * blk + kj - blk
    ok = (jnp.abs(kj - blk - qi) <= blk) & (kpos >= 0) & (kpos < seq_len)
    bias = jnp.where(ok, 0.0, NEG_BIG).astype(F32)
    top = lax.broadcasted_iota(jnp.int32, (2 * blk, 1), 0) < blk
    lo_out = lax.broadcasted_iota(jnp.int32, (2 * blk, LANES), 1) < HEAD_DIM

    def softmax_parts(s, sink_col):
        s = s + bias
        m = jnp.maximum(jnp.max(s, axis=-1, keepdims=True), sink_col)
        p = jnp.exp(s - m)
        den = jnp.sum(p, axis=-1, keepdims=True) + jnp.exp(sink_col - m)
        return p.astype(BF16), 1.0 / den

    n_slab = ATT_HEADS * HEAD_DIM // LANES
    per_kv = n_slab // ATT_KV_HEADS
    for g in range(ATT_KV_HEADS):
        slabs = [q_ref[:, (per_kv * g + t) * LANES:(per_kv * g + t + 1) * LANES] for t in range(per_kv)]
        qs = jnp.concatenate(slabs, axis=0)
        h0 = 2 * per_kv * g
        sink_e = jnp.where(top, sink_ref[h0], sink_ref[h0 + 2])
        sink_o = jnp.where(top, sink_ref[h0 + 1], sink_ref[h0 + 3])
        p_e, r_e = softmax_parts(_dot_tb(qs, k_even[g]), sink_e)
        p_o, r_o = softmax_parts(_dot_tb(qs, k_odd[g]), sink_o)
        pv = _dot(jnp.concatenate([p_e, p_o], axis=1), jnp.concatenate([v_even[g], v_odd[g]], axis=0))
        o = pv * jnp.where(lo_out, r_e, r_o)
        for t in range(per_kv):
            o_ref[:, (per_kv * g + t) * LANES:(per_kv * g + t + 1) * LANES] = o[t * blk:(t + 1) * blk].astype(BF16)


def _attn(q, k2, v2, sink, batch, seq_len):
    blk = ATT_BLOCK
    nb = seq_len // blk
    cq = q.shape[-1]
    ck = k2.shape[-1]
    q3 = q.reshape(batch, seq_len, cq)
    k3 = k2.reshape(batch, seq_len, ck)
    v3 = v2.reshape(batch, seq_len, ck)
    prev = pl.BlockSpec((None, blk, ck), lambda b, i: (b, jnp.maximum(i - 1, 0), 0))
    cur = pl.BlockSpec((None, blk, ck), lambda b, i: (b, i, 0))
    nxt = pl.BlockSpec((None, blk, ck), lambda b, i: (b, jnp.minimum(i + 1, nb - 1), 0))
    qspec = pl.BlockSpec((None, blk, cq), lambda b, i: (b, i, 0))
    out = pl.pallas_call(
        functools.partial(_attn_kernel, seq_len=seq_len),
        grid=(batch, nb),
        in_specs=[pl.BlockSpec(memory_space=pltpu.SMEM), qspec, prev, cur, nxt, prev, cur, nxt],
        out_specs=qspec,
        out_shape=jax.ShapeDtypeStruct((batch, seq_len, cq), BF16),
        compiler_params=pltpu.CompilerParams(dimension_semantics=("parallel", "parallel"),
                                             vmem_limit_bytes=VMEM_LIMIT_BYTES),
        name="attn",
    )(sink, q3, k3, k3, k3, v3, v3, v3)
    return out.reshape(batch * seq_len, cq)


def _hgrn_consts(reverse):
    c, sub = HG_CHUNK, HG_SUB
    n_sub = c // sub
    t = np.arange(c)[:, None]
    s = np.arange(c)[None, :]
    same = (t // sub) == (s // sub)
    upto = (s >= t) if reverse else (s <= t)
    coeff = np.tile((same & upto).astype(np.float32), (1, 2))
    order = list(range(n_sub))[::-1] if reverse else list(range(n_sub))
    cols = [np.broadcast_to((t // sub) == order[a], (c, c)) for a in range(1, n_sub)]
    cols.append(same & upto)
    rmask = np.tile(np.concatenate(cols, axis=1), (HG_SLAB_HEADS, 1)).astype(np.float32)
    return coeff, rmask


def _hgrn_block(q, fpre, v, lb, coeff, rmask, st_ref, reverse):
    c, sub = HG_CHUNK, HG_SUB
    n_sub = c // sub
    n_chunks = q.shape[0] // c
    hk = q.shape[1]
    n_slab = hk // HG_SLAB
    chunks = list(range(n_chunks))[::-1] if reverse else list(range(n_chunks))
    slabs = [slice(j * HG_SLAB, (j + 1) * HG_SLAB) for j in range(n_slab)]
    order = list(range(n_sub))[::-1] if reverse else list(range(n_sub))
    rows = lambda m: slice(m * sub, (m + 1) * sub)
    crow = lambda ci: slice(ci * c, (ci + 1) * c)
    low_half = lax.broadcasted_iota(jnp.int32, (c, LANES), 1) < HG_DIM
    zeros_sub = jnp.zeros((sub, hk), F32)
    zeros_tile = jnp.zeros((c, LANES), BF16)

    def head_tiles(x, j):
        return [(x[:, j * HG_SLAB + (h // 2) * LANES: j * HG_SLAB + (h // 2 + 1) * LANES], h % 2 == 0)
                for h in range(HG_SLAB_HEADS)]

    def keep_half(tile, low):
        return jnp.where(low_half, tile, 0.0) if low else jnp.where(low_half, 0.0, tile)

    def block_diag(tiles):
        blocks = [jnp.concatenate([t, zeros_tile] if h < 2 else [zeros_tile, t], axis=1)
                  for h, t in enumerate(tiles)]
        return jnp.concatenate(blocks, axis=0)

    f = lb + (1.0 - lb) / (1.0 + jnp.exp2(fpre * (-LOG2E)))
    g = jnp.log2(f)
    kk = 1.0 - f
    g_hi = g.astype(BF16)
    g_lo = (g - g_hi.astype(F32)).astype(BF16)

    def running_sums(n, after=None):
        cf = coeff
        if after is not None:
            bits = pltpu.bitcast(after[0:8, 0:LANES], jnp.uint32)
            zero = pltpu.bitcast(lax.shift_right_logical(lax.shift_right_logical(bits, jnp.uint32(16)), jnp.uint32(16)), F32)
            cf = coeff + jnp.concatenate([zero] * (c // 8), axis=0)
        ci = chunks[n]
        return _dot(cf.astype(BF16), jnp.concatenate([g_hi[crow(ci)], g_lo[crow(ci)]], axis=0))

    def prepare(n, b_rel):
        ci = chunks[n]
        t_row = []
        for a in range(n_sub):
            last = order[a] * sub + (0 if reverse else sub - 1)
            t_row.append(b_rel[last:last + 1])
        e_row = [jnp.exp2(t) for t in t_row]
        q_rel = q[crow(ci)] * jnp.exp2(b_rel)
        k_diag = kk[crow(ci)] * jnp.exp2(-b_rel)
        k_rest = [None] * n_sub
        for a in range(n_sub):
            m = order[a]
            k_rest[m] = kk[crow(ci)][rows(m)] * jnp.exp2(t_row[a] - b_rel[rows(m)])
        p_out, acc = [None] * n_sub, None
        for a in range(n_sub):
            p_out[a] = acc
            acc = e_row[a] if acc is None else acc * e_row[a]
        gamma = acc
        p_in, acc = [None] * n_sub, None
        for a in reversed(range(n_sub)):
            p_in[a] = acc
            acc = e_row[a] if acc is None else acc * e_row[a]
        q_in, k_end = [None] * n_sub, [None] * n_sub
        for a in range(n_sub):
            m = order[a]
            q_in[m] = q_rel[rows(m)] if p_out[a] is None else q_rel[rows(m)] * p_out[a]
            k_end[m] = k_rest[m] if p_in[a] is None else k_rest[m] * p_in[a]
        variants = []
        for a1 in range(1, n_sub):
            parts, fac = [zeros_sub] * n_sub, None
            for a0 in range(a1 - 1, -1, -1):
                kr = k_rest[order[a0]]
                parts[order[a0]] = kr if fac is None else kr * fac
                fac = e_row[a0] if fac is None else fac * e_row[a0]
            variants.append(jnp.concatenate(parts, axis=0))
        variants.append(k_diag)
        return dict(
            q_rel=q_rel, gamma=gamma,
            q_in=jnp.concatenate(q_in, axis=0).astype(BF16),
            k_end=jnp.concatenate(k_end, axis=0).astype(BF16),
            k_stack=jnp.concatenate(variants, axis=0).astype(BF16))

    def score(p, j):
        lhs = block_diag([keep_half(t, low).astype(BF16) for t, low in head_tiles(p["q_rel"], j)])
        return (_dot_tb(lhs, p["k_stack"][:, slabs[j]]) * rmask).astype(BF16)

    def values(n):
        ci = chunks[n]
        o_all = [_dot(scores[n][j], jnp.concatenate([v[crow(ci), ls]] * n_sub, axis=0))
                 for j, ls in enumerate(slabs)]
        incr = [_dot_ta(v[crow(ci), ls], prep[n]["k_end"][:, ls]) for ls in slabs]
        return o_all, incr

    st = [[st_ref[j * HG_SLAB_HEADS + h] for h in range(HG_SLAB_HEADS)] for j in range(n_slab)]

    def finish(n):
        o_all, incr = vals[n]
        o_slabs = []
        for j, ls in enumerate(slabs):
            o_inter = _dot_tb(prep[n]["q_in"][:, ls], block_diag([t.astype(BF16) for t in st[j]]))
            gam = head_tiles(prep[n]["gamma"], j)
            for h in range(HG_SLAB_HEADS):
                tile = incr[j][h * HG_DIM:(h + 1) * HG_DIM, (h // 2) * LANES:(h // 2 + 1) * LANES]
                st[j][h] = st[j][h] * gam[h][0] + keep_half(tile, h % 2 == 0)
            oa = o_all[j]
            o_intra = jnp.concatenate(
                [jnp.where(low_half, oa[0:c, :LANES], oa[c:2 * c, :LANES]),
                 jnp.where(low_half, oa[2 * c:3 * c, LANES:], oa[3 * c:4 * c, LANES:])], axis=1)
            o_slabs.append(o_intra + o_inter)
        return jnp.concatenate(o_slabs, axis=1)

    prep, scores, vals = [None] * n_chunks, [None] * n_chunks, [None] * n_chunks
    outs = [None] * n_chunks
    n_stage = 4
    for it in range(n_chunks + n_stage - 1):
        if it < n_chunks:
            gate = vals[it - 3][0][0] if it >= 3 else None
            prep[it] = prepare(it, running_sums(it, gate))
        if 0 <= it - 1 < n_chunks:
            scores[it - 1] = [score(prep[it - 1], j) for j in range(n_slab)]
        if 0 <= it - 2 < n_chunks:
            vals[it - 2] = values(it - 2)
        if 0 <= it - 3 < n_chunks:
            outs[chunks[it - 3]] = finish(it - 3)
    for j in range(n_slab):
        for h in range(HG_SLAB_HEADS):
            st_ref[j * HG_SLAB_HEADS + h] = st[j][h]
    return jnp.concatenate(outs, axis=0)


def _hgrn_fwd_kernel(q_ref, f_ref, v_ref, lb_ref, coeff_ref, rmask_ref, o_ref, st_ref):
    @pl.when(pl.program_id(1) == 0)
    def _():
        st_ref[...] = jnp.zeros_like(st_ref)

    o_ref[...] = _hgrn_block(q_ref[...], f_ref[...], v_ref[...], lb_ref[...], coeff_ref[...],
                             rmask_ref[...], st_ref, False)


def _hgrn_bwd_kernel(q_ref, f_ref, v_ref, lb_ref, coeff_ref, rmask_ref, of_ref, gate_ref, ng_ref, ones_ref,
                     o_ref, st_ref):
    @pl.when(pl.program_id(1) == 0)
    def _():
        st_ref[...] = jnp.zeros_like(st_ref)

    o = of_ref[...] + _hgrn_block(q_ref[...], f_ref[...], v_ref[...], lb_ref[...], coeff_ref[...],
                                  rmask_ref[...], st_ref, True)
    o2 = (o * o).astype(BF16)
    ones_bd = ones_ref[...]
    ms = jnp.concatenate([_dot(o2[:, j * HG_SLAB:(j + 1) * HG_SLAB], ones_bd)
                          for j in range(o.shape[1] // HG_SLAB)], axis=1) * (1.0 / HG_DIM)
    on = o * lax.rsqrt(ms + RMS_EPS) * ng_ref[...]
    o_ref[...] = (on * gate_ref[...].astype(F32)).astype(BF16)


def _hgrn(hq, hf_f, hf_b, hi, hg_gate, lower, norm_g, batch, seq_len):
    hk = hq.shape[-1]
    tb = HG_BLOCK
    nblk = seq_len // tb
    r3 = lambda a: a.reshape(batch, seq_len, a.shape[-1])
    const = lambda a: pl.BlockSpec(a.shape, lambda b, j: (0,) * a.ndim)
    fwd_blk = pl.BlockSpec((None, tb, hk), lambda b, j: (b, j, 0))
    bwd_blk = pl.BlockSpec((None, tb, hk), lambda b, j: (b, nblk - 1 - j, 0))
    cparams = pltpu.CompilerParams(dimension_semantics=("parallel", "arbitrary"),
                                   vmem_limit_bytes=VMEM_LIMIT_BYTES)
    state = pltpu.VMEM((hk // HG_DIM, HG_DIM, LANES), F32)
    lb_f, lb_b = lower[0:1], lower[1:2]
    consts_f = [jnp.asarray(a, F32) for a in _hgrn_consts(False)]
    consts_b = [jnp.asarray(a, F32) for a in _hgrn_consts(True)]
    head_of = np.arange(HG_SLAB) // HG_DIM
    ones_bd = jnp.asarray(head_of[:, None] == head_of[None, :], BF16)
    o_f = pl.pallas_call(
        _hgrn_fwd_kernel,
        grid=(batch, nblk),
        in_specs=[fwd_blk, fwd_blk, fwd_blk, const(lb_f)] + [const(a) for a in consts_f],
        out_specs=fwd_blk,
        out_shape=jax.ShapeDtypeStruct((batch, seq_len, hk), F32),
        scratch_shapes=[state],
        compiler_params=cparams,
        name="hgrn_fwd",
    )(r3(hq), r3(hf_f), r3(hi), lb_f, *consts_f)
    ng = jnp.tile(norm_g.astype(F32), HG_HEADS)[None, :]
    o = pl.pallas_call(
        _hgrn_bwd_kernel,
        grid=(batch, nblk),
        in_specs=[bwd_blk, bwd_blk, bwd_blk, const(lb_b)] + [const(a) for a in consts_b]
                 + [bwd_blk, bwd_blk, const(ng), const(ones_bd)],
        out_specs=bwd_blk,
        out_shape=jax.ShapeDtypeStruct((batch, seq_len, hk), BF16),
        scratch_shapes=[state],
        compiler_params=cparams,
        name="hgrn_bwd",
    )(r3(hq), r3(hf_b), r3(hi), lb_b, *consts_b, o_f, r3(hg_gate), ng, ones_bd)
    return o.reshape(batch * seq_len, hk)


def _layer_norm(y, g, b):
    mu = jnp.mean(y, axis=-1, keepdims=True)
    d = y - mu
    var = jnp.mean(d * d, axis=-1, keepdims=True)
    return d * lax.rsqrt(var + LN_EPS) * g + b


def _mix_ffn_kernel(x_ref, oa_ref, oh_ref, ga_ref, gb_ref, wpa_ref, wph_ref, wout_ref, w1_ref, w2_ref,
                    g1_ref, b1_ref, g2_ref, b2_ref, y_ref):
    mixed = (ga_ref[...].astype(F32) * _dot(oa_ref[...], wpa_ref[...])
             + gb_ref[...].astype(F32) * _dot(oh_ref[...], wph_ref[...]))
    x1 = _layer_norm(ALPHA * x_ref[...] + _dot(mixed.astype(BF16), wout_ref[...]), g1_ref[...], b1_ref[...])
    h = jnp.maximum(_dot(x1.astype(BF16), w1_ref[...]), 0.0)
    ff = _dot((h * h).astype(BF16), w2_ref[...])
    y_ref[...] = _layer_norm(ALPHA * x1 + ff, g2_ref[...], b2_ref[...])


def _mix_ffn(x2d, o_att, o_hg, ga, gb, wpa, wph, wout, w1, w2, g1, b1, g2, b2):
    n_tok, d_model = x2d.shape
    tm = FFN_ROWS
    row = lambda c: pl.BlockSpec((tm, c), lambda i: (i, 0))
    resident = lambda a: pl.BlockSpec(a.shape, lambda i: (0, 0), pipeline_mode=pl.Buffered(1))
    vec = lambda a: pl.BlockSpec((1, a.shape[-1]), lambda i: (0, 0))
    v2 = lambda a: a.reshape(1, -1).astype(F32)
    return pl.pallas_call(
        _mix_ffn_kernel,
        grid=(n_tok // tm,),
        in_specs=[row(d_model), row(o_att.shape[1]), row(o_hg.shape[1]), row(d_model), row(d_model),
                  resident(wpa), resident(wph), resident(wout), resident(w1), resident(w2),
                  vec(g1), vec(b1), vec(g2), vec(b2)],
        out_specs=row(d_model),
        out_shape=jax.ShapeDtypeStruct((n_tok, d_model), F32),
        compiler_params=pltpu.CompilerParams(dimension_semantics=("parallel",),
                                             vmem_limit_bytes=VMEM_LIMIT_BYTES),
        name="mix_ffn",
    )(x2d, o_att, o_hg, ga, gb, wpa, wph, wout, w1, w2, v2(g1), v2(b1), v2(g2), v2(b2))


def _rope_tables(seq_len):
    half = ROPE_DIM // 2
    inv = ROPE_THETA ** (-jnp.arange(0, ROPE_DIM, 2, dtype=F32) / ROPE_DIM)
    ang = jnp.arange(seq_len, dtype=F32)[:, None] * inv[None, :]
    cos, sin = jnp.cos(ang), jnp.sin(ang)
    ones = jnp.ones((seq_len, HEAD_DIM - ROPE_DIM), F32)
    zeros_rest = jnp.zeros((seq_len, HEAD_DIM - ROPE_DIM), F32)
    zeros_half = jnp.zeros((seq_len, half), F32)
    c = jnp.concatenate([cos, cos, ones], axis=1)
    s1 = jnp.concatenate([zeros_half, sin, zeros_rest], axis=1)
    s2 = jnp.concatenate([-sin, zeros_half, zeros_rest], axis=1)
    rep = LANES // HEAD_DIM
    return tuple(jnp.tile(t, (1, rep)) for t in (c, s1, s2))


def _trunk(x, params, lower, rope_tabs):
    batch, seq_len, d_model = x.shape
    x2d = x.reshape(batch * seq_len, d_model)
    for l in range(DEPTH):
        p = params[l]
        q, k2, v2, hq, hf_f, hf_b, hi, hg_gate, ga, gb = _proj(x2d, p["w_in"], rope_tabs, seq_len, d_model)
        o_att = _attn(q, k2, v2, p["sink"], batch, seq_len)
        o_hg = _hgrn(hq, hf_f, hf_b, hi, hg_gate, lower[l], p["norm_g"], batch, seq_len)
        x2d = _mix_ffn(x2d, o_att, o_hg, ga, gb, p["wpa"], p["wph"], p["wout"], p["w1"], p["w2"],
                       p["g1"], p["b1"], p["g2"], p["b2"])
    return x2d.reshape(batch, seq_len, d_model)


def kernel(x_prompt, x_sample, w_in, att_sink, hgrn_lb, hgrn_norm_g, w_proj_att, w_proj_hgrn, w_out,
           ln1_g, ln1_b, w_ff1, w_ff2, ln2_g, ln2_b):
    sm = jax.nn.softmax(hgrn_lb.astype(F32), axis=0)
    lower = jnp.cumsum(sm, axis=0) - sm[0:1]
    params = []
    for l in range(DEPTH):
        params.append(dict(
            w_in=w_in[l].astype(BF16), sink=att_sink[l].astype(F32), norm_g=hgrn_norm_g[l],
            wpa=w_proj_att[l].astype(BF16), wph=w_proj_hgrn[l].astype(BF16), wout=w_out[l].astype(BF16),
            w1=w_ff1[l].astype(BF16), w2=w_ff2[l].astype(BF16),
            g1=ln1_g[l], b1=ln1_b[l], g2=ln2_g[l], b2=ln2_b[l]))
    rope_tabs = _rope_tables(x_prompt.shape[1])
    y_prompt = _trunk(x_prompt, params, lower, rope_tabs)
    if x_sample.shape[1] != x_prompt.shape[1]:
        rope_tabs = _rope_tables(x_sample.shape[1])
    y_sample = _trunk(x_sample, params, lower, rope_tabs)
    return (y_prompt, y_sample)
```

```python
import functools

import jax
import jax.numpy as jnp
import numpy as np
from jax import lax
from jax.experimental import pallas as pl
from jax.experimental.pallas import tpu as pltpu

F32 = jnp.float32
BF16 = jnp.bfloat16

ATT_HEADS = 8
ATT_KV_HEADS = 2
HEAD_DIM = 64
ATT_BLOCK = 128
ROPE_THETA = 500000.0
ROPE_DIM = HEAD_DIM // 4
HG_HEADS = 8
HG_DIM = 64
DEPTH = 2
ALPHA = (2 * DEPTH) ** 0.25
LN_EPS = 1e-5
RMS_EPS = 1e-6
NEG_BIG = -1e30
LOG2E = 1.4426950408889634

LANES = 128
MXU_DIM = 256
VMEM_LIMIT_BYTES = 56 * 1024 * 1024

PROJ_ROWS = 512
FFN_ROWS = 512
ATT_TILE = 512
HG_BLOCK = 512
HG_CHUNK = 64
HG_SUB = 16
HG_SLAB = MXU_DIM
HG_SLAB_HEADS = HG_SLAB // HG_DIM


def _sigmoid(x):
    return 1.0 / (1.0 + jnp.exp(-x))


def _dot(a, b):
    return jnp.dot(a, b, preferred_element_type=F32)


def _dot_tb(a, b):
    return lax.dot_general(a, b, (((1,), (1,)), ((), ())), preferred_element_type=F32)


def _dot_ta(a, b):
    return lax.dot_general(a, b, (((0,), (0,)), ((), ())), preferred_element_type=F32)


def _proj_kernel(x_ref, w_ref, rc_ref, rs1_ref, rs2_ref,
                 q_ref, k_ref, v_ref, hq_ref, ff_ref, fb_ref, hi_ref, hg_ref, ga_ref, gb_ref,
                 *, cuts):
    xb = x_ref[...].astype(BF16)

    def mm(name):
        a, b = cuts[name]
        return _dot(xb, w_ref[:, a:b])

    rc = rc_ref[...]
    rs1 = rs1_ref[...]
    rs2 = rs2_ref[...]

    def rope(blk):
        return blk * rc + pltpu.roll(blk, ROPE_DIM // 2, 1) * rs1 + pltpu.roll(blk, LANES - ROPE_DIM // 2, 1) * rs2

    aq = mm("aq")
    for j in range(aq.shape[1] // LANES):
        blk = rope(aq[:, j * LANES:(j + 1) * LANES]) * (LOG2E * HEAD_DIM ** -0.5)
        q_ref[:, j * LANES:(j + 1) * LANES] = blk.astype(BF16)
    ak = rope(mm("ak"))
    k_ref[:, :LANES] = ak.astype(BF16)
    k_ref[:, LANES:] = pltpu.roll(ak, HEAD_DIM, 1).astype(BF16)
    av = mm("av")
    v_ref[:, :LANES] = av.astype(BF16)
    v_ref[:, LANES:] = pltpu.roll(av, HEAD_DIM, 1).astype(BF16)

    hq = mm("hq")
    hq_ref[...] = hq * _sigmoid(hq) * (HG_DIM ** -0.5)
    ff_ref[...] = mm("hf_f")
    fb_ref[...] = mm("hf_b")
    hi_ref[...] = mm("hi").astype(BF16)
    hg = mm("hg")
    hg_ref[...] = (hg * _sigmoid(hg)).astype(BF16)
    ga_ref[...] = _sigmoid(mm("ga")).astype(BF16)
    gb_ref[...] = _sigmoid(mm("gb")).astype(BF16)


def _proj(x2d, w_in, rope_tabs, seq_len, d_model):
    n_tok = x2d.shape[0]
    tm = PROJ_ROWS
    att_q = ATT_HEADS * HEAD_DIM
    att_kv = ATT_KV_HEADS * HEAD_DIM
    hg = HG_HEADS * HG_DIM
    sizes = [("aq", att_q), ("ak", att_kv), ("av", att_kv), ("hq", hg), ("hf_f", hg), ("hf_b", hg),
             ("hi", hg), ("hg", hg), ("ga", d_model), ("gb", d_model)]
    cuts, off = {}, 0
    for name, s in sizes:
        cuts[name] = (off, off + s)
        off += s
    d_in = off
    pos_blocks = seq_len // tm
    row = lambda c: pl.BlockSpec((tm, c), lambda i: (i, 0))
    tab = pl.BlockSpec((tm, LANES), lambda i: (i % pos_blocks, 0))
    out_shapes = [
        jax.ShapeDtypeStruct((n_tok, att_q), BF16),
        jax.ShapeDtypeStruct((n_tok, 2 * att_kv), BF16),
        jax.ShapeDtypeStruct((n_tok, 2 * att_kv), BF16),
        jax.ShapeDtypeStruct((n_tok, hg), F32),
        jax.ShapeDtypeStruct((n_tok, hg), F32),
        jax.ShapeDtypeStruct((n_tok, hg), F32),
        jax.ShapeDtypeStruct((n_tok, hg), BF16),
        jax.ShapeDtypeStruct((n_tok, hg), BF16),
        jax.ShapeDtypeStruct((n_tok, d_model), BF16),
        jax.ShapeDtypeStruct((n_tok, d_model), BF16),
    ]
    return pl.pallas_call(
        functools.partial(_proj_kernel, cuts=cuts),
        grid=(n_tok // tm,),
        in_specs=[row(d_model),
                  pl.BlockSpec((d_model, d_in), lambda i: (0, 0), pipeline_mode=pl.Buffered(1)),
                  tab, tab, tab],
        out_specs=[row(s.shape[1]) for s in out_shapes],
        out_shape=out_shapes,
        compiler_params=pltpu.CompilerParams(dimension_semantics=("parallel",),
                                             vmem_limit_bytes=VMEM_LIMIT_BYTES),
        name="proj",
    )(x2d, w_in, *rope_tabs)


def _attn_bias_table():
    blk = ATT_BLOCK
    qi = np.arange(blk)[:, None]
    kj = np.arange(blk)[None, :]
    prev_ok = kj >= qi
    next_ok = kj <= qi
    none = np.zeros((blk, blk), bool)
    tab = []
    for p_ok, n_ok in ((prev_ok, next_ok), (none, next_ok), (prev_ok, none)):
        m = np.concatenate([p_ok, n_ok], axis=1)
        tab.append(np.where(np.concatenate([m, m], axis=0), 0.0, NEG_BIG))
    return np.stack(tab).astype(np.float32)


def _attn_kernel(sink_ref, q_ref, kp_ref, kc_ref, kn_ref, vp_ref, vc_ref, vn_ref,
                 bias_first_ref, bias_mid_ref, bias_last_ref, o_ref):
    blk = ATT_BLOCK
    n_qb = q_ref.shape[0] // blk
    k = jnp.concatenate([kp_ref[...], kc_ref[...], kn_ref[...]], axis=0)
    v = jnp.concatenate([vp_ref[...], vc_ref[...], vn_ref[...]], axis=0)
    rows_kv = k.shape[0]
    lo = lax.broadcasted_iota(jnp.int32, (rows_kv, LANES), 1) < HEAD_DIM
    zero = jnp.zeros((rows_kv, LANES), BF16)
    k_arr, k_rot = k[:, :LANES], k[:, LANES:]
    v_arr, v_rot = v[:, :LANES], v[:, LANES:]
    k_even = [jnp.where(lo, k_arr, zero), jnp.where(lo, k_rot, zero)]
    k_odd = [jnp.where(lo, zero, k_rot), jnp.where(lo, zero, k_arr)]
    v_even = [jnp.where(lo, v_arr, zero), jnp.where(lo, v_rot, zero)]
    v_odd = [jnp.where(lo, zero, v_rot), jnp.where(lo, zero, v_arr)]

    top = lax.broadcasted_iota(jnp.int32, (2 * blk, 1), 0) < blk
    lo_out = lax.broadcasted_iota(jnp.int32, (2 * blk, LANES), 1) < HEAD_DIM

    def softmax_parts(s, bias, sink_col):
        s_prev = s[:, :blk] + bias[:, :blk]
        s_mid = s[:, blk:2 * blk]
        s_next = s[:, 2 * blk:] + bias[:, blk:]
        m = jnp.max(jnp.maximum(jnp.maximum(s_prev, s_mid), s_next), axis=-1, keepdims=True)
        m = jnp.maximum(m, sink_col)
        p_prev, p_mid, p_next = jnp.exp2(s_prev - m), jnp.exp2(s_mid - m), jnp.exp2(s_next - m)
        den = jnp.sum(p_prev + p_mid + p_next, axis=-1, keepdims=True) + jnp.exp2(sink_col - m)
        return jnp.concatenate([p_prev, p_mid, p_next], axis=1).astype(BF16), 1.0 / den

    n_slab = ATT_HEADS * HEAD_DIM // LANES
    per_kv = n_slab // ATT_KV_HEADS
    for t in range(n_qb):
        bias = (bias_first_ref if t == 0 else bias_last_ref if t == n_qb - 1 else bias_mid_ref)[...]
        qr = slice(t * blk, (t + 1) * blk)
        kr = slice(t * blk, (t + 3) * blk)
        for g in range(ATT_KV_HEADS):
            slabs = [q_ref[qr, (per_kv * g + u) * LANES:(per_kv * g + u + 1) * LANES] for u in range(per_kv)]
            qs = jnp.concatenate(slabs, axis=0)
            h0 = 2 * per_kv * g
            sink_e = jnp.where(top, sink_ref[h0], sink_ref[h0 + 2]) * LOG2E
            sink_o = jnp.where(top, sink_ref[h0 + 1], sink_ref[h0 + 3]) * LOG2E
            p_e, r_e = softmax_parts(_dot_tb(qs, k_even[g][kr]), bias, sink_e)
            p_o, r_o = softmax_parts(_dot_tb(qs, k_odd[g][kr]), bias, sink_o)
            pv = _dot(jnp.concatenate([p_e, p_o], axis=1),
                      jnp.concatenate([v_even[g][kr], v_odd[g][kr]], axis=0))
            o = pv * jnp.where(lo_out, r_e, r_o)
            for u in range(per_kv):
                o_ref[qr, (per_kv * g + u) * LANES:(per_kv * g + u + 1) * LANES] = \
                    o[u * blk:(u + 1) * blk].astype(BF16)


def _attn(q, k2, v2, sink, batch, seq_len):
    blk = ATT_BLOCK
    tq = ATT_TILE
    per = tq // blk
    nt = seq_len // tq
    nb = seq_len // blk
    assert nb >= 2
    cq = q.shape[-1]
    ck = k2.shape[-1]
    q3 = q.reshape(batch, seq_len, cq)
    k3 = k2.reshape(batch, seq_len, ck)
    v3 = v2.reshape(batch, seq_len, ck)
    prev = pl.BlockSpec((None, blk, ck), lambda b, i: (b, jnp.maximum(i * per - 1, 0), 0))
    cur = pl.BlockSpec((None, tq, ck), lambda b, i: (b, i, 0))
    nxt = pl.BlockSpec((None, blk, ck), lambda b, i: (b, jnp.minimum((i + 1) * per, nb - 1), 0))
    qspec = pl.BlockSpec((None, tq, cq), lambda b, i: (b, i, 0))
    bias_tab = jnp.asarray(_attn_bias_table())
    bshape = (None,) + bias_tab.shape[1:]
    bias_first = pl.BlockSpec(bshape, lambda b, i: (jnp.where(i == 0, 1, 0), 0, 0))
    bias_mid = pl.BlockSpec(bshape, lambda b, i: (0, 0, 0))
    bias_last = pl.BlockSpec(bshape, lambda b, i: (jnp.where(i == nt - 1, 2, 0), 0, 0))
    out = pl.pallas_call(
        _attn_kernel,
        grid=(batch, nt),
        in_specs=[pl.BlockSpec(memory_space=pltpu.SMEM), qspec, prev, cur, nxt, prev, cur, nxt,
                  bias_first, bias_mid, bias_last],
        out_specs=qspec,
        out_shape=jax.ShapeDtypeStruct((batch, seq_len, cq), BF16),
        compiler_params=pltpu.CompilerParams(dimension_semantics=("parallel", "parallel"),
                                             vmem_limit_bytes=VMEM_LIMIT_BYTES),
        name="attn",
    )(sink, q3, k3, k3, k3, v3, v3, v3, bias_tab, bias_tab, bias_tab)
    return out.reshape(batch * seq_len, cq)


def _hgrn_consts(reverse):
    c, sub = HG_CHUNK, HG_SUB
    n_sub = c // sub
    t = np.arange(c)[:, None]
    s = np.arange(c)[None, :]
    same = (t // sub) == (s // sub)
    upto = (s >= t) if reverse else (s <= t)
    coeff = np.tile((same & upto).astype(np.float32), (1, 2))
    order = list(range(n_sub))[::-1] if reverse else list(range(n_sub))
    cols = [np.broadcast_to((t // sub) == order[a], (c, c)) for a in range(1, n_sub)]
    cols.append(same & upto)
    rmask = np.tile(np.concatenate(cols, axis=1), (HG_SLAB_HEADS, 1)).astype(np.float32)
    return coeff, rmask


def _hgrn_block(q, fpre, v, lb, coeff, rmask, st_ref, reverse):
    c, sub = HG_CHUNK, HG_SUB
    n_sub = c // sub
    n_chunks = q.shape[0] // c
    hk = q.shape[1]
    n_slab = hk // HG_SLAB
    chunks = list(range(n_chunks))[::-1] if reverse else list(range(n_chunks))
    slabs = [slice(j * HG_SLAB, (j + 1) * HG_SLAB) for j in range(n_slab)]
    order = list(range(n_sub))[::-1] if reverse else list(range(n_sub))
    rows = lambda m: slice(m * sub, (m + 1) * sub)
    crow = lambda ci: slice(ci * c, (ci + 1) * c)
    low_half = lax.broadcasted_iota(jnp.int32, (c, LANES), 1) < HG_DIM
    zeros_sub = jnp.zeros((sub, hk), F32)
    zeros_tile = jnp.zeros((c, LANES), BF16)

    def head_tiles(x, j):
        return [(x[:, j * HG_SLAB + (h // 2) * LANES: j * HG_SLAB + (h // 2 + 1) * LANES], h % 2 == 0)
                for h in range(HG_SLAB_HEADS)]

    def keep_half(tile, low):
        return jnp.where(low_half, tile, 0.0) if low else jnp.where(low_half, 0.0, tile)

    def block_diag(tiles):
        blocks = [jnp.concatenate([t, zeros_tile] if h < 2 else [zeros_tile, t], axis=1)
                  for h, t in enumerate(tiles)]
        return jnp.concatenate(blocks, axis=0)

    f = lb + (1.0 - lb) / (1.0 + jnp.exp2(fpre * (-LOG2E)))
    g = jnp.log2(f)
    kk = 1.0 - f
    g_hi = g.astype(BF16)
    g_lo = (g - g_hi.astype(F32)).astype(BF16)

    def running_sums(n, after=None):
        cf = coeff
        if after is not None:
            bits = pltpu.bitcast(after[0:8, 0:LANES], jnp.uint32)
            zero = pltpu.bitcast(lax.shift_right_logical(lax.shift_right_logical(bits, jnp.uint32(16)), jnp.uint32(16)), F32)
            cf = coeff + jnp.concatenate([zero] * (c // 8), axis=0)
        ci = chunks[n]
        return _dot(cf.astype(BF16), jnp.concatenate([g_hi[crow(ci)], g_lo[crow(ci)]], axis=0))

    def prepare(n, b_rel):
        ci = chunks[n]
        t_row = []
        for a in range(n_sub):
            last = order[a] * sub + (0 if reverse else sub - 1)
            t_row.append(b_rel[last:last + 1])
        e_row = [jnp.exp2(t) for t in t_row]
        q_rel = q[crow(ci)] * jnp.exp2(b_rel)
        k_diag = kk[crow(ci)] * jnp.exp2(-b_rel)
        k_rest = [None] * n_sub
        for a in range(n_sub):
            m = order[a]
            k_rest[m] = kk[crow(ci)][rows(m)] * jnp.exp2(t_row[a] - b_rel[rows(m)])
        p_out, acc = [None] * n_sub, None
        for a in range(n_sub):
            p_out[a] = acc
            acc = e_row[a] if acc is None else acc * e_row[a]
        gamma = acc
        p_in, acc = [None] * n_sub, None
        for a in reversed(range(n_sub)):
            p_in[a] = acc
            acc = e_row[a] if acc is None else acc * e_row[a]
        q_in, k_end = [None] * n_sub, [None] * n_sub
        for a in range(n_sub):
            m = order[a]
            q_in[m] = q_rel[rows(m)] if p_out[a] is None else q_rel[rows(m)] * p_out[a]
            k_end[m] = k_rest[m] if p_in[a] is None else k_rest[m] * p_in[a]
        variants = []
        for a1 in range(1, n_sub):
            parts, fac = [zeros_sub] * n_sub, None
            for a0 in range(a1 - 1, -1, -1):
                kr = k_rest[order[a0]]
                parts[order[a0]] = kr if fac is None else kr * fac
                fac = e_row[a0] if fac is None else fac * e_row[a0]
            variants.append(jnp.concatenate(parts, axis=0))
        variants.append(k_diag)
        return dict(
            q_rel=q_rel, gamma=gamma,
            q_in=jnp.concatenate(q_in, axis=0).astype(BF16),
            k_end=jnp.concatenate(k_end, axis=0).astype(BF16),
            k_stack=jnp.concatenate(variants, axis=0).astype(BF16))

    def score(p, j):
        lhs = block_diag([keep_half(t, low).astype(BF16) for t, low in head_tiles(p["q_rel"], j)])
        return (_dot_tb(lhs, p["k_stack"][:, slabs[j]]) * rmask).astype(BF16)

    def values(n):
        ci = chunks[n]
        o_all = [_dot(scores[n][j], jnp.concatenate([v[crow(ci), ls]] * n_sub, axis=0))
                 for j, ls in enumerate(slabs)]
        incr = [_dot_ta(v[crow(ci), ls], prep[n]["k_end"][:, ls]) for ls in slabs]
        return o_all, incr

    st = [[st_ref[j * HG_SLAB_HEADS + h] for h in range(HG_SLAB_HEADS)] for j in range(n_slab)]

    def finish(n):
        o_all, incr = vals[n]
        o_slabs = []
        for j, ls in enumerate(slabs):
            o_inter = _dot_tb(prep[n]["q_in"][:, ls], block_diag([t.astype(BF16) for t in st[j]]))
            gam = head_tiles(prep[n]["gamma"], j)
            for h in range(HG_SLAB_HEADS):
                tile = incr[j][h * HG_DIM:(h + 1) * HG_DIM, (h // 2) * LANES:(h // 2 + 1) * LANES]
                st[j][h] = st[j][h] * gam[h][0] + keep_half(tile, h % 2 == 0)
            oa = o_all[j]
            o_intra = jnp.concatenate(
                [jnp.where(low_half, oa[0:c, :LANES], oa[c:2 * c, :LANES]),
                 jnp.where(low_half, oa[2 * c:3 * c, LANES:], oa[3 * c:4 * c, LANES:])], axis=1)
            o_slabs.append(o_intra + o_inter)
        return jnp.concatenate(o_slabs, axis=1)

    prep, scores, vals = [None] * n_chunks, [None] * n_chunks, [None] * n_chunks
    outs = [None] * n_chunks
    n_stage = 4
    for it in range(n_chunks + n_stage - 1):
        if it < n_chunks:
            gate = vals[it - 3][0][0] if it >= 3 else None
            prep[it] = prepare(it, running_sums(it, gate))
        if 0 <= it - 1 < n_chunks:
            scores[it - 1] = [score(prep[it - 1], j) for j in range(n_slab)]
        if 0 <= it - 2 < n_chunks:
            vals[it - 2] = values(it - 2)
        if 0 <= it - 3 < n_chunks:
            outs[chunks[it - 3]] = finish(it - 3)
    for j in range(n_slab):
        for h in range(HG_SLAB_HEADS):
            st_ref[j * HG_SLAB_HEADS + h] = st[j][h]
    return jnp.concatenate(outs, axis=0)


def _hgrn_fwd_kernel(q_ref, f_ref, v_ref, lb_ref, coeff_ref, rmask_ref, o_ref, st_ref):
    @pl.when(pl.program_id(1) == 0)
    def _():
        st_ref[...] = jnp.zeros_like(st_ref)

    o_ref[...] = _hgrn_block(q_ref[...], f_ref[...], v_ref[...], lb_ref[...], coeff_ref[...],
                             rmask_ref[...], st_ref, False)


def _hgrn_bwd_kernel(q_ref, f_ref, v_ref, lb_ref, coeff_ref, rmask_ref, of_ref, gate_ref, ng_ref, ones_ref,
                     o_ref, st_ref):
    @pl.when(pl.program_id(1) == 0)
    def _():
        st_ref[...] = jnp.zeros_like(st_ref)

    o = of_ref[...] + _hgrn_block(q_ref[...], f_ref[...], v_ref[...], lb_ref[...], coeff_ref[...],
                                  rmask_ref[...], st_ref, True)
    o2 = (o * o).astype(BF16)
    ones_bd = ones_ref[...]
    ms = jnp.concatenate([_dot(o2[:, j * HG_SLAB:(j + 1) * HG_SLAB], ones_bd)
                          for j in range(o.shape[1] // HG_SLAB)], axis=1) * (1.0 / HG_DIM)
    on = o * lax.rsqrt(ms + RMS_EPS) * ng_ref[...]
    o_ref[...] = (on * gate_ref[...].astype(F32)).astype(BF16)


def _hgrn(hq, hf_f, hf_b, hi, hg_gate, lower, norm_g, batch, seq_len):
    hk = hq.shape[-1]
    tb = HG_BLOCK
    nblk = seq_len // tb
    r3 = lambda a: a.reshape(batch, seq_len, a.shape[-1])
    const = lambda a: pl.BlockSpec(a.shape, lambda b, j: (0,) * a.ndim)
    fwd_blk = pl.BlockSpec((None, tb, hk), lambda b, j: (b, j, 0))
    bwd_blk = pl.BlockSpec((None, tb, hk), lambda b, j: (b, nblk - 1 - j, 0))
    cparams = pltpu.CompilerParams(dimension_semantics=("parallel", "arbitrary"),
                                   vmem_limit_bytes=VMEM_LIMIT_BYTES)
    state = pltpu.VMEM((hk // HG_DIM, HG_DIM, LANES), F32)
    lb_f, lb_b = lower[0:1], lower[1:2]
    consts_f = [jnp.asarray(a, F32) for a in _hgrn_consts(False)]
    consts_b = [jnp.asarray(a, F32) for a in _hgrn_consts(True)]
    head_of = np.arange(HG_SLAB) // HG_DIM
    ones_bd = jnp.asarray(head_of[:, None] == head_of[None, :], BF16)
    o_f = pl.pallas_call(
        _hgrn_fwd_kernel,
        grid=(batch, nblk),
        in_specs=[fwd_blk, fwd_blk, fwd_blk, const(lb_f)] + [const(a) for a in consts_f],
        out_specs=fwd_blk,
        out_shape=jax.ShapeDtypeStruct((batch, seq_len, hk), F32),
        scratch_shapes=[state],
        compiler_params=cparams,
        name="hgrn_fwd",
    )(r3(hq), r3(hf_f), r3(hi), lb_f, *consts_f)
    ng = jnp.tile(norm_g.astype(F32), HG_HEADS)[None, :]
    o = pl.pallas_call(
        _hgrn_bwd_kernel,
        grid=(batch, nblk),
        in_specs=[bwd_blk, bwd_blk, bwd_blk, const(lb_b)] + [const(a) for a in consts_b]
                 + [bwd_blk, bwd_blk, const(ng), const(ones_bd)],
        out_specs=bwd_blk,
        out_shape=jax.ShapeDtypeStruct((batch, seq_len, hk), BF16),
        scratch_shapes=[state],
        compiler_params=cparams,
        name="hgrn_bwd",
    )(r3(hq), r3(hf_b), r3(hi), lb_b, *consts_b, o_f, r3(hg_gate), ng, ones_bd)
    return o.reshape(batch * seq_len, hk)


def _layer_norm(y, g, b):
    mu = jnp.mean(y, axis=-1, keepdims=True)
    d = y - mu
    var = jnp.mean(d * d, axis=-1, keepdims=True)
    return d * lax.rsqrt(var + LN_EPS) * g + b


def _mix_ffn_kernel(x_ref, oa_ref, oh_ref, ga_ref, gb_ref, wpa_ref, wph_ref, wout_ref, w1_ref, w2_ref,
                    g1_ref, b1_ref, g2_ref, b2_ref, y_ref):
    mixed = (ga_ref[...].astype(F32) * _dot(oa_ref[...], wpa_ref[...])
             + gb_ref[...].astype(F32) * _dot(oh_ref[...], wph_ref[...]))
    x1 = _layer_norm(ALPHA * x_ref[...] + _dot(mixed.astype(BF16), wout_ref[...]), g1_ref[...], b1_ref[...])
    h = jnp.maximum(_dot(x1.astype(BF16), w1_ref[...]), 0.0)
    ff = _dot((h * h).astype(BF16), w2_ref[...])
    y_ref[...] = _layer_norm(ALPHA * x1 + ff, g2_ref[...], b2_ref[...])


def _mix_ffn(x2d, o_att, o_hg, ga, gb, wpa, wph, wout, w1, w2, g1, b1, g2, b2):
    n_tok, d_model = x2d.shape
    tm = FFN_ROWS
    row = lambda c: pl.BlockSpec((tm, c), lambda i: (i, 0))
    resident = lambda a: pl.BlockSpec(a.shape, lambda i: (0, 0), pipeline_mode=pl.Buffered(1))
    vec = lambda a: pl.BlockSpec((1, a.shape[-1]), lambda i: (0, 0))
    v2 = lambda a: a.reshape(1, -1).astype(F32)
    return pl.pallas_call(
        _mix_ffn_kernel,
        grid=(n_tok // tm,),
        in_specs=[row(d_model), row(o_att.shape[1]), row(o_hg.shape[1]), row(d_model), row(d_model),
                  resident(wpa), resident(wph), resident(wout), resident(w1), resident(w2),
                  vec(g1), vec(b1), vec(g2), vec(b2)],
        out_specs=row(d_model),
        out_shape=jax.ShapeDtypeStruct((n_tok, d_model), F32),
        compiler_params=pltpu.CompilerParams(dimension_semantics=("parallel",),
                                             vmem_limit_bytes=VMEM_LIMIT_BYTES),
        name="mix_ffn",
    )(x2d, o_att, o_hg, ga, gb, wpa, wph, wout, w1, w2, v2(g1), v2(b1), v2(g2), v2(b2))


def _rope_tables(seq_len):
    half = ROPE_DIM // 2
    inv = ROPE_THETA ** (-jnp.arange(0, ROPE_DIM, 2, dtype=F32) / ROPE_DIM)
    ang = jnp.arange(seq_len, dtype=F32)[:, None] * inv[None, :]
    cos, sin = jnp.cos(ang), jnp.sin(ang)
    ones = jnp.ones((seq_len, HEAD_DIM - ROPE_DIM), F32)
    zeros_rest = jnp.zeros((seq_len, HEAD_DIM - ROPE_DIM), F32)
    zeros_half = jnp.zeros((seq_len, half), F32)
    c = jnp.concatenate([cos, cos, ones], axis=1)
    s1 = jnp.concatenate([zeros_half, sin, zeros_rest], axis=1)
    s2 = jnp.concatenate([-sin, zeros_half, zeros_rest], axis=1)
    rep = LANES // HEAD_DIM
    return tuple(jnp.tile(t, (1, rep)) for t in (c, s1, s2))


def _trunk(x, params, lower, rope_tabs):
    batch, seq_len, d_model = x.shape
    x2d = x.reshape(batch * seq_len, d_model)
    for l in range(DEPTH):
        p = params[l]
        q, k2, v2, hq, hf_f, hf_b, hi, hg_gate, ga, gb = _proj(x2d, p["w_in"], rope_tabs, seq_len, d_model)
        o_att = _attn(q, k2, v2, p["sink"], batch, seq_len)
        o_hg = _hgrn(hq, hf_f, hf_b, hi, hg_gate, lower[l], p["norm_g"], batch, seq_len)
        x2d = _mix_ffn(x2d, o_att, o_hg, ga, gb, p["wpa"], p["wph"], p["wout"], p["w1"], p["w2"],
                       p["g1"], p["b1"], p["g2"], p["b2"])
    return x2d.reshape(batch, seq_len, d_model)


def kernel(x_prompt, x_sample, w_in, att_sink, hgrn_lb, hgrn_norm_g, w_proj_att, w_proj_hgrn, w_out,
           ln1_g, ln1_b, w_ff1, w_ff2, ln2_g, ln2_b):
    sm = jax.nn.softmax(hgrn_lb.astype(F32), axis=0)
    lower = jnp.cumsum(sm, axis=0) - sm[0:1]
    params = []
    for l in range(DEPTH):
        params.append(dict(
            w_in=w_in[l].astype(BF16), sink=att_sink[l].astype(F32), norm_g=hgrn_norm_g[l],
            wpa=w_proj_att[l].astype(BF16), wph=w_proj_hgrn[l].astype(BF16), wout=w_out[l].astype(BF16),
            w1=w_ff1[l].astype(BF16), w2=w_ff2[l].astype(BF16),
            g1=ln1_g[l], b1=ln1_b[l], g2=ln2_g[l], b2=ln2_b[l]))
    rope_tabs = _rope_tables(x_prompt.shape[1])
    y_prompt = _trunk(x_prompt, params, lower, rope_tabs)
    if x_sample.shape[1] != x_prompt.shape[1]:
        rope_tabs = _rope_tables(x_sample.shape[1])
    y_sample = _trunk(x_sample, params, lower, rope_tabs)
    return (y_prompt, y_sample)
```

```python
import functools

import jax
import jax.numpy as jnp
import numpy as np
from jax import lax
from jax.experimental import pallas as pl
from jax.experimental.pallas import tpu as pltpu

F32 = jnp.float32
BF16 = jnp.bfloat16

ATT_HEADS = 8
ATT_KV_HEADS = 2
HEAD_DIM = 64
ATT_BLOCK = 128
ROPE_THETA = 500000.0
ROPE_DIM = HEAD_DIM // 4
HG_HEADS = 8
HG_DIM = 64
DEPTH = 2
ALPHA = (2 * DEPTH) ** 0.25
LN_EPS = 1e-5
RMS_EPS = 1e-6
NEG_BIG = -1e30
LOG2E = 1.4426950408889634

LANES = 128
MXU_DIM = 256
VMEM_LIMIT_BYTES = 56 * 1024 * 1024

PROJ_ROWS = 512
FFN_ROWS = 512
ATT_TILE = 512
HG_BLOCK = 512
HG_CHUNK = 64
HG_SUB = 16
HG_SLAB = MXU_DIM
HG_SLAB_HEADS = HG_SLAB // HG_DIM


def _sigmoid(x):
    return 1.0 / (1.0 + jnp.exp(-x))


def _sigmoid2(x):
    return 1.0 / (1.0 + jnp.exp2(x * (-LOG2E)))


def _dot(a, b):
    return jnp.dot(a, b, preferred_element_type=F32)


def _dot_tb(a, b):
    return lax.dot_general(a, b, (((1,), (1,)), ((), ())), preferred_element_type=F32)


def _dot_ta(a, b):
    return lax.dot_general(a, b, (((0,), (0,)), ((), ())), preferred_element_type=F32)


def _proj_kernel(x_ref, w_ref, rc_ref, rs1_ref, rs2_ref, lb_ref,
                 q_ref, k_ref, v_ref, hq_ref, gf_ref, kf_ref, gb2_ref, kb_ref, hi_ref, hg_ref, ga_ref, gb_ref,
                 *, cuts):
    xb = x_ref[...].astype(BF16)

    def mm(name):
        a, b = cuts[name]
        return _dot(xb, w_ref[:, a:b])

    rc = rc_ref[...]
    rs1 = rs1_ref[...]
    rs2 = rs2_ref[...]

    def rope(blk):
        return blk * rc + pltpu.roll(blk, ROPE_DIM // 2, 1) * rs1 + pltpu.roll(blk, LANES - ROPE_DIM // 2, 1) * rs2

    def mm_tile(name, t):
        a, _ = cuts[name]
        return _dot(xb, w_ref[:, a + t * MXU_DIM:a + (t + 1) * MXU_DIM]), slice(t * MXU_DIM, (t + 1) * MXU_DIM)

    def do_q(t):
        y, cs = mm_tile("aq", t)
        for j in range(MXU_DIM // LANES):
            blk = rope(y[:, j * LANES:(j + 1) * LANES]) * (LOG2E * HEAD_DIM ** -0.5)
            q_ref[:, cs.start + j * LANES:cs.start + (j + 1) * LANES] = blk.astype(BF16)

    def do_kv(t):
        akv, _ = mm_tile("akv", t)
        ak = rope(akv[:, :LANES])
        k_ref[:, :LANES] = ak.astype(BF16)
        k_ref[:, LANES:] = pltpu.roll(ak, HEAD_DIM, 1).astype(BF16)
        av = akv[:, LANES:]
        v_ref[:, :LANES] = av.astype(BF16)
        v_ref[:, LANES:] = pltpu.roll(av, HEAD_DIM, 1).astype(BF16)

    def do_hq(t):
        y, cs = mm_tile("hq", t)
        hq_ref[:, cs] = y * _sigmoid2(y) * (HG_DIM ** -0.5)

    def do_gates(name, row, g_ref, k_ref, t):
        y, cs = mm_tile(name, t)
        lb = lb_ref[row:row + 1, cs]
        f = lb + (1.0 - lb) * _sigmoid2(y)
        g = jnp.log2(f)
        g_hi = g.astype(BF16)
        hk = k_ref.shape[1]
        g_ref[:, cs] = g_hi
        g_ref[:, hk + cs.start:hk + cs.stop] = (g - g_hi.astype(F32)).astype(BF16)
        k_ref[:, cs] = (1.0 - f).astype(BF16)

    def do_hi(t):
        y, cs = mm_tile("hi", t)
        hi_ref[:, cs] = y.astype(BF16)

    def do_hg(t):
        y, cs = mm_tile("hg", t)
        hg_ref[:, cs] = (y * _sigmoid2(y)).astype(BF16)

    def do_mix_gate(name, ref, t):
        y, cs = mm_tile(name, t)
        ref[:, cs] = _sigmoid2(y).astype(BF16)

    gate_f = functools.partial(do_gates, "hf_f", 0, gf_ref, kf_ref)
    gate_b = functools.partial(do_gates, "hf_b", 1, gb2_ref, kb_ref)
    ga_t = functools.partial(do_mix_gate, "ga", ga_ref)
    gb_t = functools.partial(do_mix_gate, "gb", gb_ref)
    plan = [(do_q, 0), (do_q, 1), (gate_f, 0), (do_kv, 0), (ga_t, 0), (gate_f, 1),
            (ga_t, 1), (gate_b, 0), (ga_t, 2), (do_hq, 0), (gate_b, 1), (ga_t, 3), (do_hq, 1),
            (do_hg, 0), (gb_t, 0), (do_hg, 1), (gb_t, 1), (gb_t, 2), (gb_t, 3), (do_hi, 0), (do_hi, 1)]
    for fn, t in plan:
        fn(t)


def _proj(x2d, w_in, rope_tabs, lower, seq_len, d_model):
    n_tok = x2d.shape[0]
    tm = PROJ_ROWS
    att_q = ATT_HEADS * HEAD_DIM
    att_kv = ATT_KV_HEADS * HEAD_DIM
    hg = HG_HEADS * HG_DIM
    sizes = [("aq", att_q), ("akv", 2 * att_kv), ("hq", hg), ("hf_f", hg), ("hf_b", hg),
             ("hi", hg), ("hg", hg), ("ga", d_model), ("gb", d_model)]
    cuts, off = {}, 0
    for name, s in sizes:
        cuts[name] = (off, off + s)
        off += s
    d_in = off
    pos_blocks = seq_len // tm
    row = lambda c: pl.BlockSpec((tm, c), lambda i: (i, 0))
    tab = pl.BlockSpec((tm, LANES), lambda i: (i % pos_blocks, 0))
    out_shapes = [
        jax.ShapeDtypeStruct((n_tok, att_q), BF16),
        jax.ShapeDtypeStruct((n_tok, 2 * att_kv), BF16),
        jax.ShapeDtypeStruct((n_tok, 2 * att_kv), BF16),
        jax.ShapeDtypeStruct((n_tok, hg), F32),
        jax.ShapeDtypeStruct((n_tok, 2 * hg), BF16),
        jax.ShapeDtypeStruct((n_tok, hg), BF16),
        jax.ShapeDtypeStruct((n_tok, 2 * hg), BF16),
        jax.ShapeDtypeStruct((n_tok, hg), BF16),
        jax.ShapeDtypeStruct((n_tok, hg), BF16),
        jax.ShapeDtypeStruct((n_tok, hg), BF16),
        jax.ShapeDtypeStruct((n_tok, d_model), BF16),
        jax.ShapeDtypeStruct((n_tok, d_model), BF16),
    ]
    return pl.pallas_call(
        functools.partial(_proj_kernel, cuts=cuts),
        grid=(n_tok // tm,),
        in_specs=[row(d_model),
                  pl.BlockSpec((d_model, d_in), lambda i: (0, 0), pipeline_mode=pl.Buffered(1)),
                  tab, tab, tab, pl.BlockSpec(lower.shape, lambda i: (0, 0))],
        out_specs=[row(s.shape[1]) for s in out_shapes],
        out_shape=out_shapes,
        compiler_params=pltpu.CompilerParams(dimension_semantics=("parallel",),
                                             vmem_limit_bytes=VMEM_LIMIT_BYTES),
        name="proj",
    )(x2d, w_in, *rope_tabs, lower)


def _attn_bias_table():
    blk = ATT_BLOCK
    qi = np.arange(blk)[:, None]
    kj = np.arange(blk)[None, :]
    prev_ok = kj >= qi
    next_ok = kj <= qi
    none = np.zeros((blk, blk), bool)
    tab = []
    for p_ok, n_ok in ((prev_ok, next_ok), (none, next_ok), (prev_ok, none)):
        m = np.concatenate([p_ok, n_ok], axis=1)
        tab.append(np.where(np.concatenate([m, m], axis=0), 0.0, NEG_BIG))
    return np.stack(tab).astype(np.float32)


def _attn_kernel(sink_ref, q_ref, kp_ref, kc_ref, kn_ref, vp_ref, vc_ref, vn_ref,
                 bias_first_ref, bias_mid_ref, bias_last_ref, o_ref):
    blk = ATT_BLOCK
    n_qb = q_ref.shape[0] // blk
    k = jnp.concatenate([kp_ref[...], kc_ref[...], kn_ref[...]], axis=0)
    v = jnp.concatenate([vp_ref[...], vc_ref[...], vn_ref[...]], axis=0)
    rows_kv = k.shape[0]
    lo = lax.broadcasted_iota(jnp.int32, (rows_kv, LANES), 1) < HEAD_DIM
    zero = jnp.zeros((rows_kv, LANES), BF16)
    k_arr, k_rot = k[:, :LANES], k[:, LANES:]
    v_arr, v_rot = v[:, :LANES], v[:, LANES:]
    k_even = [jnp.where(lo, k_arr, zero), jnp.where(lo, k_rot, zero)]
    k_odd = [jnp.where(lo, zero, k_rot), jnp.where(lo, zero, k_arr)]
    v_even = [jnp.where(lo, v_arr, zero), jnp.where(lo, v_rot, zero)]
    v_odd = [jnp.where(lo, zero, v_rot), jnp.where(lo, zero, v_arr)]

    top = lax.broadcasted_iota(jnp.int32, (2 * blk, 1), 0) < blk
    lo_out = lax.broadcasted_iota(jnp.int32, (2 * blk, LANES), 1) < HEAD_DIM

    def softmax_parts(s, bias, sink_col):
        s_prev = s[:, :blk] + bias[:, :blk]
        s_mid = s[:, blk:2 * blk]
        s_next = s[:, 2 * blk:] + bias[:, blk:]
        m = jnp.max(jnp.maximum(jnp.maximum(s_prev, s_mid), s_next), axis=-1, keepdims=True)
        m = jnp.maximum(m, sink_col)
        p_prev, p_mid, p_next = jnp.exp2(s_prev - m), jnp.exp2(s_mid - m), jnp.exp2(s_next - m)
        den = jnp.sum(p_prev + p_mid + p_next, axis=-1, keepdims=True) + jnp.exp2(sink_col - m)
        return jnp.concatenate([p_prev, p_mid, p_next], axis=1).astype(BF16), 1.0 / den

    n_slab = ATT_HEADS * HEAD_DIM // LANES
    per_kv = n_slab // ATT_KV_HEADS
    for t in range(n_qb):
        bias = (bias_first_ref if t == 0 else bias_last_ref if t == n_qb - 1 else bias_mid_ref)[...]
        qr = slice(t * blk, (t + 1) * blk)
        kr = slice(t * blk, (t + 3) * blk)
        for g in range(ATT_KV_HEADS):
            slabs = [q_ref[qr, (per_kv * g + u) * LANES:(per_kv * g + u + 1) * LANES] for u in range(per_kv)]
            qs = jnp.concatenate(slabs, axis=0)
            h0 = 2 * per_kv * g
            sink_e = jnp.where(top, sink_ref[h0], sink_ref[h0 + 2]) * LOG2E
            sink_o = jnp.where(top, sink_ref[h0 + 1], sink_ref[h0 + 3]) * LOG2E
            p_e, r_e = softmax_parts(_dot_tb(qs, k_even[g][kr]), bias, sink_e)
            p_o, r_o = softmax_parts(_dot_tb(qs, k_odd[g][kr]), bias, sink_o)
            pv = _dot(jnp.concatenate([p_e, p_o], axis=1),
                      jnp.concatenate([v_even[g][kr], v_odd[g][kr]], axis=0))
            o = pv * jnp.where(lo_out, r_e, r_o)
            for u in range(per_kv):
                o_ref[qr, (per_kv * g + u) * LANES:(per_kv * g + u + 1) * LANES] = \
                    o[u * blk:(u + 1) * blk].astype(BF16)


def _attn(q, k2, v2, sink, batch, seq_len):
    blk = ATT_BLOCK
    tq = ATT_TILE
    per = tq // blk
    nt = seq_len // tq
    nb = seq_len // blk
    assert nb >= 2
    cq = q.shape[-1]
    ck = k2.shape[-1]
    q3 = q.reshape(batch, seq_len, cq)
    k3 = k2.reshape(batch, seq_len, ck)
    v3 = v2.reshape(batch, seq_len, ck)
    prev = pl.BlockSpec((None, blk, ck), lambda b, i: (b, jnp.maximum(i * per - 1, 0), 0))
    cur = pl.BlockSpec((None, tq, ck), lambda b, i: (b, i, 0))
    nxt = pl.BlockSpec((None, blk, ck), lambda b, i: (b, jnp.minimum((i + 1) * per, nb - 1), 0))
    qspec = pl.BlockSpec((None, tq, cq), lambda b, i: (b, i, 0))
    bias_tab = jnp.asarray(_attn_bias_table())
    bshape = (None,) + bias_tab.shape[1:]
    bias_first = pl.BlockSpec(bshape, lambda b, i: (jnp.where(i == 0, 1, 0), 0, 0))
    bias_mid = pl.BlockSpec(bshape, lambda b, i: (0, 0, 0))
    bias_last = pl.BlockSpec(bshape, lambda b, i: (jnp.where(i == nt - 1, 2, 0), 0, 0))
    out = pl.pallas_call(
        _attn_kernel,
        grid=(batch, nt),
        in_specs=[pl.BlockSpec(memory_space=pltpu.SMEM), qspec, prev, cur, nxt, prev, cur, nxt,
                  bias_first, bias_mid, bias_last],
        out_specs=qspec,
        out_shape=jax.ShapeDtypeStruct((batch, seq_len, cq), BF16),
        compiler_params=pltpu.CompilerParams(dimension_semantics=("parallel", "parallel"),
                                             vmem_limit_bytes=VMEM_LIMIT_BYTES),
        name="attn",
    )(sink, q3, k3, k3, k3, v3, v3, v3, bias_tab, bias_tab, bias_tab)
    return out.reshape(batch * seq_len, cq)


def _hgrn_consts(reverse):
    c, sub = HG_CHUNK, HG_SUB
    n_sub = c // sub
    t = np.arange(c)[:, None]
    s = np.arange(c)[None, :]
    same = (t // sub) == (s // sub)
    upto = (s >= t) if reverse else (s <= t)
    coeff = np.tile((same & upto).astype(np.float32), (1, 2))
    order = list(range(n_sub))[::-1] if reverse else list(range(n_sub))
    cols = [np.broadcast_to((t // sub) == order[a], (c, c)) for a in range(1, n_sub)]
    cols.append(same & upto)
    rmask = np.tile(np.concatenate(cols, axis=1), (HG_SLAB_HEADS, 1)).astype(np.float32)
    return coeff, rmask


def _hgrn_block(q, g_split, kk, v, coeff, rmask, st_ref, reverse):
    c, sub = HG_CHUNK, HG_SUB
    n_sub = c // sub
    n_chunks = q.shape[0] // c
    hk = q.shape[1]
    n_slab = hk // HG_SLAB
    chunks = list(range(n_chunks))[::-1] if reverse else list(range(n_chunks))
    slabs = [slice(j * HG_SLAB, (j + 1) * HG_SLAB) for j in range(n_slab)]
    order = list(range(n_sub))[::-1] if reverse else list(range(n_sub))
    rows = lambda m: slice(m * sub, (m + 1) * sub)
    crow = lambda ci: slice(ci * c, (ci + 1) * c)
    low_half = lax.broadcasted_iota(jnp.int32, (c, LANES), 1) < HG_DIM
    zeros_sub = jnp.zeros((sub, hk), F32)
    zeros_tile = jnp.zeros((c, LANES), BF16)

    def head_tiles(x, j):
        return [(x[:, j * HG_SLAB + (h // 2) * LANES: j * HG_SLAB + (h // 2 + 1) * LANES], h % 2 == 0)
                for h in range(HG_SLAB_HEADS)]

    def keep_half(tile, low):
        return jnp.where(low_half, tile, 0.0) if low else jnp.where(low_half, 0.0, tile)

    def block_diag(tiles):
        blocks = [jnp.concatenate([t, zeros_tile] if h < 2 else [zeros_tile, t], axis=1)
                  for h, t in enumerate(tiles)]
        return jnp.concatenate(blocks, axis=0)

    g_hi, g_lo = g_split[:, :hk], g_split[:, hk:]

    def running_sums(n, after=None):
        cf = coeff
        if after is not None:
            bits = pltpu.bitcast(after[0:8, 0:LANES], jnp.uint32)
            zero = pltpu.bitcast(lax.shift_right_logical(lax.shift_right_logical(bits, jnp.uint32(16)), jnp.uint32(16)), F32)
            cf = coeff + jnp.concatenate([zero] * (c // 8), axis=0)
        ci = chunks[n]
        return _dot(cf.astype(BF16), jnp.concatenate([g_hi[crow(ci)], g_lo[crow(ci)]], axis=0))

    def prepare(n, b_rel):
        ci = chunks[n]
        t_row = []
        for a in range(n_sub):
            last = order[a] * sub + (0 if reverse else sub - 1)
            t_row.append(b_rel[last:last + 1])
        e_row = [jnp.exp2(t) for t in t_row]
        q_rel = q[crow(ci)] * jnp.exp2(b_rel)
        kk_c = kk[crow(ci)].astype(F32)
        k_diag = kk_c * jnp.exp2(-b_rel)
        k_rest = [None] * n_sub
        for a in range(n_sub):
            m = order[a]
            k_rest[m] = kk_c[rows(m)] * jnp.exp2(t_row[a] - b_rel[rows(m)])
        p_out, acc = [None] * n_sub, None
        for a in range(n_sub):
            p_out[a] = acc
            acc = e_row[a] if acc is None else acc * e_row[a]
        gamma = acc
        p_in, acc = [None] * n_sub, None
        for a in reversed(range(n_sub)):
            p_in[a] = acc
            acc = e_row[a] if acc is None else acc * e_row[a]
        q_in, k_end = [None] * n_sub, [None] * n_sub
        for a in range(n_sub):
            m = order[a]
            q_in[m] = q_rel[rows(m)] if p_out[a] is None else q_rel[rows(m)] * p_out[a]
            k_end[m] = k_rest[m] if p_in[a] is None else k_rest[m] * p_in[a]
        variants = []
        for a1 in range(1, n_sub):
            parts, fac = [zeros_sub] * n_sub, None
            for a0 in range(a1 - 1, -1, -1):
                kr = k_rest[order[a0]]
                parts[order[a0]] = kr if fac is None else kr * fac
                fac = e_row[a0] if fac is None else fac * e_row[a0]
            variants.append(jnp.concatenate(parts, axis=0))
        variants.append(k_diag)
        return dict(
            q_rel=q_rel, gamma=gamma,
            q_in=jnp.concatenate(q_in, axis=0).astype(BF16),
            k_end=jnp.concatenate(k_end, axis=0).astype(BF16),
            k_stack=jnp.concatenate(variants, axis=0).astype(BF16))

    def score(p, j):
        lhs = block_diag([keep_half(t, low).astype(BF16) for t, low in head_tiles(p["q_rel"], j)])
        return (_dot_tb(lhs, p["k_stack"][:, slabs[j]]) * rmask).astype(BF16)

    def values(n):
        ci = chunks[n]
        o_all = [_dot(scores[n][j], jnp.concatenate([v[crow(ci), ls]] * n_sub, axis=0))
                 for j, ls in enumerate(slabs)]
        incr = [_dot_ta(v[crow(ci), ls], prep[n]["k_end"][:, ls]) for ls in slabs]
        return o_all, incr

    st = [[st_ref[j * HG_SLAB_HEADS + h] for h in range(HG_SLAB_HEADS)] for j in range(n_slab)]

    def finish(n):
        o_all, incr = vals[n]
        o_slabs = []
        for j, ls in enumerate(slabs):
            o_inter = _dot_tb(prep[n]["q_in"][:, ls], block_diag([t.astype(BF16) for t in st[j]]))
            gam = head_tiles(prep[n]["gamma"], j)
            for h in range(HG_SLAB_HEADS):
                tile = incr[j][h * HG_DIM:(h + 1) * HG_DIM, (h // 2) * LANES:(h // 2 + 1) * LANES]
                st[j][h] = st[j][h] * gam[h][0] + keep_half(tile, h % 2 == 0)
            oa = o_all[j]
            o_intra = jnp.concatenate(
                [jnp.where(low_half, oa[0:c, :LANES], oa[c:2 * c, :LANES]),
                 jnp.where(low_half, oa[2 * c:3 * c, LANES:], oa[3 * c:4 * c, LANES:])], axis=1)
            o_slabs.append(o_intra + o_inter)
        return jnp.concatenate(o_slabs, axis=1)

    prep, scores, vals = [None] * n_chunks, [None] * n_chunks, [None] * n_chunks
    outs = [None] * n_chunks
    n_stage = 4
    for it in range(n_chunks + n_stage - 1):
        if it < n_chunks:
            gate = vals[it - 3][0][0] if it >= 3 else None
            prep[it] = prepare(it, running_sums(it, gate))
        if 0 <= it - 1 < n_chunks:
            scores[it - 1] = [score(prep[it - 1], j) for j in range(n_slab)]
        if 0 <= it - 2 < n_chunks:
            vals[it - 2] = values(it - 2)
        if 0 <= it - 3 < n_chunks:
            outs[chunks[it - 3]] = finish(it - 3)
    for j in range(n_slab):
        for h in range(HG_SLAB_HEADS):
            st_ref[j * HG_SLAB_HEADS + h] = st[j][h]
    return jnp.concatenate(outs, axis=0)


def _hgrn_fwd_kernel(q_ref, g_ref, k_ref, v_ref, coeff_ref, rmask_ref, o_ref, st_ref):
    @pl.when(pl.program_id(1) == 0)
    def _():
        st_ref[...] = jnp.zeros_like(st_ref)

    o_ref[...] = _hgrn_block(q_ref[...], g_ref[...], k_ref[...], v_ref[...], coeff_ref[...],
                             rmask_ref[...], st_ref, False)


def _hgrn_bwd_kernel(q_ref, g_ref, k_ref, v_ref, coeff_ref, rmask_ref, of_ref, gate_ref, ng_ref, ones_ref,
                     o_ref, st_ref):
    @pl.when(pl.program_id(1) == 0)
    def _():
        st_ref[...] = jnp.zeros_like(st_ref)

    o = of_ref[...] + _hgrn_block(q_ref[...], g_ref[...], k_ref[...], v_ref[...], coeff_ref[...],
                                  rmask_ref[...], st_ref, True)
    o2 = (o * o).astype(BF16)
    ones_bd = ones_ref[...]
    ms = jnp.concatenate([_dot(o2[:, j * HG_SLAB:(j + 1) * HG_SLAB], ones_bd)
                          for j in range(o.shape[1] // HG_SLAB)], axis=1) * (1.0 / HG_DIM)
    on = o * lax.rsqrt(ms + RMS_EPS) * ng_ref[...]
    o_ref[...] = (on * gate_ref[...].astype(F32)).astype(BF16)


def _hgrn(hq, g_f, k_f, g_b, k_b, hi, hg_gate, norm_g, batch, seq_len):
    hk = hq.shape[-1]
    tb = HG_BLOCK
    nblk = seq_len // tb
    r3 = lambda a: a.reshape(batch, seq_len, a.shape[-1])
    const = lambda a: pl.BlockSpec(a.shape, lambda b, j: (0,) * a.ndim)
    fwd = lambda c: pl.BlockSpec((None, tb, c), lambda b, j: (b, j, 0))
    bwd = lambda c: pl.BlockSpec((None, tb, c), lambda b, j: (b, nblk - 1 - j, 0))
    fwd_blk, bwd_blk = fwd(hk), bwd(hk)
    cparams = pltpu.CompilerParams(dimension_semantics=("parallel", "arbitrary"),
                                   vmem_limit_bytes=VMEM_LIMIT_BYTES)
    state = pltpu.VMEM((hk // HG_DIM, HG_DIM, LANES), F32)
    consts_f = [jnp.asarray(a, F32) for a in _hgrn_consts(False)]
    consts_b = [jnp.asarray(a, F32) for a in _hgrn_consts(True)]
    head_of = np.arange(HG_SLAB) // HG_DIM
    ones_bd = jnp.asarray(head_of[:, None] == head_of[None, :], BF16)
    o_f = pl.pallas_call(
        _hgrn_fwd_kernel,
        grid=(batch, nblk),
        in_specs=[fwd_blk, fwd(2 * hk), fwd_blk, fwd_blk] + [const(a) for a in consts_f],
        out_specs=fwd_blk,
        out_shape=jax.ShapeDtypeStruct((batch, seq_len, hk), F32),
        scratch_shapes=[state],
        compiler_params=cparams,
        name="hgrn_fwd",
    )(r3(hq), r3(g_f), r3(k_f), r3(hi), *consts_f)
    ng = jnp.tile(norm_g.astype(F32), HG_HEADS)[None, :]
    o = pl.pallas_call(
        _hgrn_bwd_kernel,
        grid=(batch, nblk),
        in_specs=[bwd_blk, bwd(2 * hk), bwd_blk, bwd_blk] + [const(a) for a in consts_b]
                 + [bwd_blk, bwd_blk, const(ng), const(ones_bd)],
        out_specs=bwd_blk,
        out_shape=jax.ShapeDtypeStruct((batch, seq_len, hk), BF16),
        scratch_shapes=[state],
        compiler_params=cparams,
        name="hgrn_bwd",
    )(r3(hq), r3(g_b), r3(k_b), r3(hi), *consts_b, o_f, r3(hg_gate), ng, ones_bd)
    return o.reshape(batch * seq_len, hk)


def _layer_norm(y, g, b):
    mu = jnp.mean(y, axis=-1, keepdims=True)
    d = y - mu
    var = jnp.mean(d * d, axis=-1, keepdims=True)
    return d * lax.rsqrt(var + LN_EPS) * g + b


def _mix_ffn_kernel(x_ref, oa_ref, oh_ref, ga_ref, gb_ref, wpa_ref, wph_ref, wout_ref, w1_ref, w2_ref,
                    g1_ref, b1_ref, g2_ref, b2_ref, y_ref):
    mixed = (ga_ref[...].astype(F32) * _dot(oa_ref[...], wpa_ref[...])
             + gb_ref[...].astype(F32) * _dot(oh_ref[...], wph_ref[...]))
    x1 = _layer_norm(ALPHA * x_ref[...] + _dot(mixed.astype(BF16), wout_ref[...]), g1_ref[...], b1_ref[...])
    h = jnp.maximum(_dot(x1.astype(BF16), w1_ref[...]), 0.0)
    ff = _dot((h * h).astype(BF16), w2_ref[...])
    y_ref[...] = _layer_norm(ALPHA * x1 + ff, g2_ref[...], b2_ref[...])


def _mix_ffn(x2d, o_att, o_hg, ga, gb, wpa, wph, wout, w1, w2, g1, b1, g2, b2):
    n_tok, d_model = x2d.shape
    tm = FFN_ROWS
    row = lambda c: pl.BlockSpec((tm, c), lambda i: (i, 0))
    resident = lambda a: pl.BlockSpec(a.shape, lambda i: (0, 0), pipeline_mode=pl.Buffered(1))
    vec = lambda a: pl.BlockSpec((1, a.shape[-1]), lambda i: (0, 0))
    v2 = lambda a: a.reshape(1, -1).astype(F32)
    return pl.pallas_call(
        _mix_ffn_kernel,
        grid=(n_tok // tm,),
        in_specs=[row(d_model), row(o_att.shape[1]), row(o_hg.shape[1]), row(d_model), row(d_model),
                  resident(wpa), resident(wph), resident(wout), resident(w1), resident(w2),
                  vec(g1), vec(b1), vec(g2), vec(b2)],
        out_specs=row(d_model),
        out_shape=jax.ShapeDtypeStruct((n_tok, d_model), F32),
        compiler_params=pltpu.CompilerParams(dimension_semantics=("parallel",),
                                             vmem_limit_bytes=VMEM_LIMIT_BYTES),
        name="mix_ffn",
    )(x2d, o_att, o_hg, ga, gb, wpa, wph, wout, w1, w2, v2(g1), v2(b1), v2(g2), v2(b2))


def _rope_tables(seq_len):
    half = ROPE_DIM // 2
    inv = ROPE_THETA ** (-jnp.arange(0, ROPE_DIM, 2, dtype=F32) / ROPE_DIM)
    ang = jnp.arange(seq_len, dtype=F32)[:, None] * inv[None, :]
    cos, sin = jnp.cos(ang), jnp.sin(ang)
    ones = jnp.ones((seq_len, HEAD_DIM - ROPE_DIM), F32)
    zeros_rest = jnp.zeros((seq_len, HEAD_DIM - ROPE_DIM), F32)
    zeros_half = jnp.zeros((seq_len, half), F32)
    c = jnp.concatenate([cos, cos, ones], axis=1)
    s1 = jnp.concatenate([zeros_half, sin, zeros_rest], axis=1)
    s2 = jnp.concatenate([-sin, zeros_half, zeros_rest], axis=1)
    rep = LANES // HEAD_DIM
    return tuple(jnp.tile(t, (1, rep)) for t in (c, s1, s2))


def _trunk(x, params, lower, rope_tabs):
    batch, seq_len, d_model = x.shape
    x2d = x.reshape(batch * seq_len, d_model)
    for l in range(DEPTH):
        p = params[l]
        q, k2, v2, hq, g_f, k_f, g_b, k_b, hi, hg_gate, ga, gb = _proj(x2d, p["w_in"], rope_tabs, lower[l],
                                                                       seq_len, d_model)
        o_att = _attn(q, k2, v2, p["sink"], batch, seq_len)
        o_hg = _hgrn(hq, g_f, k_f, g_b, k_b, hi, hg_gate, p["norm_g"], batch, seq_len)
        x2d = _mix_ffn(x2d, o_att, o_hg, ga, gb, p["wpa"], p["wph"], p["wout"], p["w1"], p["w2"],
                       p["g1"], p["b1"], p["g2"], p["b2"])
    return x2d.reshape(batch, seq_len, d_model)


def kernel(x_prompt, x_sample, w_in, att_sink, hgrn_lb, hgrn_norm_g, w_proj_att, w_proj_hgrn, w_out,
           ln1_g, ln1_b, w_ff1, w_ff2, ln2_g, ln2_b):
    sm = jax.nn.softmax(hgrn_lb.astype(F32), axis=0)
    lower = jnp.cumsum(sm, axis=0) - sm[0:1]
    params = []
    for l in range(DEPTH):
        params.append(dict(
            w_in=w_in[l].astype(BF16), sink=att_sink[l].astype(F32), norm_g=hgrn_norm_g[l],
            wpa=w_proj_att[l].astype(BF16), wph=w_proj_hgrn[l].astype(BF16), wout=w_out[l].astype(BF16),
            w1=w_ff1[l].astype(BF16), w2=w_ff2[l].astype(BF16),
            g1=ln1_g[l], b1=ln1_b[l], g2=ln2_g[l], b2=ln2_b[l]))
    rope_tabs = _rope_tables(x_prompt.shape[1])
    y_prompt = _trunk(x_prompt, params, lower, rope_tabs)
    if x_sample.shape[1] != x_prompt.shape[1]:
        rope_tabs = _rope_tables(x_sample.shape[1])
    y_sample = _trunk(x_sample, params, lower, rope_tabs)
    return (y_prompt, y_sample)
```

```python
import functools

import jax
import jax.numpy as jnp
import numpy as np
from jax import lax
from jax.experimental import pallas as pl
from jax.experimental.pallas import tpu as pltpu

F32 = jnp.float32
BF16 = jnp.bfloat16

ATT_HEADS = 8
ATT_KV_HEADS = 2
HEAD_DIM = 64
ATT_BLOCK = 128
ROPE_THETA = 500000.0
ROPE_DIM = HEAD_DIM // 4
HG_HEADS = 8
HG_DIM = 64
DEPTH = 2
ALPHA = (2 * DEPTH) ** 0.25
LN_EPS = 1e-5
RMS_EPS = 1e-6
NEG_BIG = -1e30
LOG2E = 1.4426950408889634

LANES = 128
MXU_DIM = 256
VMEM_LIMIT_BYTES = 56 * 1024 * 1024

PROJ_ROWS = 512
FFN_ROWS = 512
ATT_TILE = 512
HG_BLOCK = 512
HG_CHUNK = 64
HG_SUB = 16
HG_SLAB = MXU_DIM
HG_SLAB_HEADS = HG_SLAB // HG_DIM
HG_SAFE_LOG2 = 96.0


def _sigmoid(x):
    return 1.0 / (1.0 + jnp.exp(-x))


def _dot(a, b):
    return jnp.dot(a, b, preferred_element_type=F32)


def _dot_tb(a, b):
    return lax.dot_general(a, b, (((1,), (1,)), ((), ())), preferred_element_type=F32)


def _dot_ta(a, b):
    return lax.dot_general(a, b, (((0,), (0,)), ((), ())), preferred_element_type=F32)


def _proj_kernel(x_ref, w_ref, rc_ref, rs1_ref, rs2_ref,
                 q_ref, k_ref, v_ref, hq_ref, ff_ref, fb_ref, hi_ref, hg_ref, ga_ref, gb_ref,
                 *, cuts):
    xb = x_ref[...].astype(BF16)

    def mm(name):
        a, b = cuts[name]
        return _dot(xb, w_ref[:, a:b])

    rc = rc_ref[...]
    rs1 = rs1_ref[...]
    rs2 = rs2_ref[...]

    def rope(blk):
        return blk * rc + pltpu.roll(blk, ROPE_DIM // 2, 1) * rs1 + pltpu.roll(blk, LANES - ROPE_DIM // 2, 1) * rs2

    aq = mm("aq")
    for j in range(aq.shape[1] // LANES):
        blk = rope(aq[:, j * LANES:(j + 1) * LANES]) * (LOG2E * HEAD_DIM ** -0.5)
        q_ref[:, j * LANES:(j + 1) * LANES] = blk.astype(BF16)
    akv = mm("akv")
    ak = rope(akv[:, :LANES])
    k_ref[:, :LANES] = ak.astype(BF16)
    k_ref[:, LANES:] = pltpu.roll(ak, HEAD_DIM, 1).astype(BF16)
    av = akv[:, LANES:]
    v_ref[:, :LANES] = av.astype(BF16)
    v_ref[:, LANES:] = pltpu.roll(av, HEAD_DIM, 1).astype(BF16)

    hq = mm("hq")
    hq_ref[...] = hq * _sigmoid(hq) * (HG_DIM ** -0.5)
    ff_ref[...] = mm("hf_f")
    fb_ref[...] = mm("hf_b")
    hi_ref[...] = mm("hi").astype(BF16)
    hg = mm("hg")
    hg_ref[...] = (hg * _sigmoid(hg)).astype(BF16)
    ga_ref[...] = _sigmoid(mm("ga")).astype(BF16)
    gb_ref[...] = _sigmoid(mm("gb")).astype(BF16)


def _proj(x2d, w_in, rope_tabs, seq_len, d_model):
    n_tok = x2d.shape[0]
    tm = PROJ_ROWS
    att_q = ATT_HEADS * HEAD_DIM
    att_kv = ATT_KV_HEADS * HEAD_DIM
    hg = HG_HEADS * HG_DIM
    sizes = [("aq", att_q), ("akv", 2 * att_kv), ("hq", hg), ("hf_f", hg), ("hf_b", hg),
             ("hi", hg), ("hg", hg), ("ga", d_model), ("gb", d_model)]
    cuts, off = {}, 0
    for name, s in sizes:
        cuts[name] = (off, off + s)
        off += s
    d_in = off
    pos_blocks = seq_len // tm
    row = lambda c: pl.BlockSpec((tm, c), lambda i: (i, 0))
    tab = pl.BlockSpec((tm, LANES), lambda i: (i % pos_blocks, 0))
    out_shapes = [
        jax.ShapeDtypeStruct((n_tok, att_q), BF16),
        jax.ShapeDtypeStruct((n_tok, 2 * att_kv), BF16),
        jax.ShapeDtypeStruct((n_tok, 2 * att_kv), BF16),
        jax.ShapeDtypeStruct((n_tok, hg), F32),
        jax.ShapeDtypeStruct((n_tok, hg), F32),
        jax.ShapeDtypeStruct((n_tok, hg), F32),
        jax.ShapeDtypeStruct((n_tok, hg), BF16),
        jax.ShapeDtypeStruct((n_tok, hg), BF16),
        jax.ShapeDtypeStruct((n_tok, d_model), BF16),
        jax.ShapeDtypeStruct((n_tok, d_model), BF16),
    ]
    return pl.pallas_call(
        functools.partial(_proj_kernel, cuts=cuts),
        grid=(n_tok // tm,),
        in_specs=[row(d_model),
                  pl.BlockSpec((d_model, d_in), lambda i: (0, 0), pipeline_mode=pl.Buffered(1)),
                  tab, tab, tab],
        out_specs=[row(s.shape[1]) for s in out_shapes],
        out_shape=out_shapes,
        compiler_params=pltpu.CompilerParams(dimension_semantics=("parallel",),
                                             vmem_limit_bytes=VMEM_LIMIT_BYTES),
        name="proj",
    )(x2d, w_in, *rope_tabs)


def _attn_bias_table():
    blk = ATT_BLOCK
    qi = np.arange(blk)[:, None]
    kj = np.arange(blk)[None, :]
    prev_ok = kj >= qi
    next_ok = kj <= qi
    none = np.zeros((blk, blk), bool)
    tab = []
    for p_ok, n_ok in ((prev_ok, next_ok), (none, next_ok), (prev_ok, none)):
        m = np.concatenate([p_ok, n_ok], axis=1)
        tab.append(np.where(np.concatenate([m, m], axis=0), 0.0, NEG_BIG))
    return np.stack(tab).astype(np.float32)


def _attn_kernel(sink_ref, q_ref, kp_ref, kc_ref, kn_ref, vp_ref, vc_ref, vn_ref,
                 bias_first_ref, bias_mid_ref, bias_last_ref, o_ref):
    blk = ATT_BLOCK
    n_qb = q_ref.shape[0] // blk
    k = jnp.concatenate([kp_ref[...], kc_ref[...], kn_ref[...]], axis=0)
    v = jnp.concatenate([vp_ref[...], vc_ref[...], vn_ref[...]], axis=0)
    rows_kv = k.shape[0]
    lo = lax.broadcasted_iota(jnp.int32, (rows_kv, LANES), 1) < HEAD_DIM
    zero = jnp.zeros((rows_kv, LANES), BF16)
    k_arr, k_rot = k[:, :LANES], k[:, LANES:]
    v_arr, v_rot = v[:, :LANES], v[:, LANES:]
    k_even = [jnp.where(lo, k_arr, zero), jnp.where(lo, k_rot, zero)]
    k_odd = [jnp.where(lo, zero, k_rot), jnp.where(lo, zero, k_arr)]
    v_even = [jnp.where(lo, v_arr, zero), jnp.where(lo, v_rot, zero)]
    v_odd = [jnp.where(lo, zero, v_rot), jnp.where(lo, zero, v_arr)]
    top = lax.broadcasted_iota(jnp.int32, (2 * blk, 1), 0) < blk
    lo_out = lax.broadcasted_iota(jnp.int32, (2 * blk, LANES), 1) < HEAD_DIM

    def softmax_parts(s, bias, sink_col):
        s_prev = s[:, :blk] + bias[:, :blk]
        s_mid = s[:, blk:2 * blk]
        s_next = s[:, 2 * blk:] + bias[:, blk:]
        m = jnp.max(jnp.maximum(jnp.maximum(s_prev, s_mid), s_next), axis=-1, keepdims=True)
        m = jnp.maximum(m, sink_col)
        p_prev, p_mid, p_next = jnp.exp2(s_prev - m), jnp.exp2(s_mid - m), jnp.exp2(s_next - m)
        den = jnp.sum(p_prev + p_mid + p_next, axis=-1, keepdims=True) + jnp.exp2(sink_col - m)
        return jnp.concatenate([p_prev, p_mid, p_next], axis=1).astype(BF16), 1.0 / den

    n_slab = ATT_HEADS * HEAD_DIM // LANES
    per_kv = n_slab // ATT_KV_HEADS
    for t in range(n_qb):
        bias = (bias_first_ref if t == 0 else bias_last_ref if t == n_qb - 1 else bias_mid_ref)[...]
        qr = slice(t * blk, (t + 1) * blk)
        kr = slice(t * blk, (t + 3) * blk)
        for g in range(ATT_KV_HEADS):
            slabs = [q_ref[qr, (per_kv * g + u) * LANES:(per_kv * g + u + 1) * LANES] for u in range(per_kv)]
            qs = jnp.concatenate(slabs, axis=0)
            h0 = 2 * per_kv * g
            sink_e = jnp.where(top, sink_ref[h0], sink_ref[h0 + 2]) * LOG2E
            sink_o = jnp.where(top, sink_ref[h0 + 1], sink_ref[h0 + 3]) * LOG2E
            p_e, r_e = softmax_parts(_dot_tb(qs, k_even[g][kr]), bias, sink_e)
            p_o, r_o = softmax_parts(_dot_tb(qs, k_odd[g][kr]), bias, sink_o)
            pv = _dot(jnp.concatenate([p_e, p_o], axis=1),
                      jnp.concatenate([v_even[g][kr], v_odd[g][kr]], axis=0))
            o = pv * jnp.where(lo_out, r_e, r_o)
            for u in range(per_kv):
                o_ref[qr, (per_kv * g + u) * LANES:(per_kv * g + u + 1) * LANES] = \
                    o[u * blk:(u + 1) * blk].astype(BF16)


def _attn(q, k2, v2, sink, batch, seq_len):
    blk = ATT_BLOCK
    tq = ATT_TILE
    per = tq // blk
    nt = seq_len // tq
    nb = seq_len // blk
    assert nb >= 2
    cq = q.shape[-1]
    ck = k2.shape[-1]
    q3 = q.reshape(batch, seq_len, cq)
    k3 = k2.reshape(batch, seq_len, ck)
    v3 = v2.reshape(batch, seq_len, ck)
    prev = pl.BlockSpec((None, blk, ck), lambda b, i: (b, jnp.maximum(i * per - 1, 0), 0))
    cur = pl.BlockSpec((None, tq, ck), lambda b, i: (b, i, 0))
    nxt = pl.BlockSpec((None, blk, ck), lambda b, i: (b, jnp.minimum((i + 1) * per, nb - 1), 0))
    qspec = pl.BlockSpec((None, tq, cq), lambda b, i: (b, i, 0))
    bias_tab = jnp.asarray(_attn_bias_table())
    bshape = (None,) + bias_tab.shape[1:]
    bias_first = pl.BlockSpec(bshape, lambda b, i: (jnp.where(i == 0, 1, 0), 0, 0))
    bias_mid = pl.BlockSpec(bshape, lambda b, i: (0, 0, 0))
    bias_last = pl.BlockSpec(bshape, lambda b, i: (jnp.where(i == nt - 1, 2, 0), 0, 0))
    out = pl.pallas_call(
        _attn_kernel,
        grid=(batch, nt),
        in_specs=[pl.BlockSpec(memory_space=pltpu.SMEM), qspec, prev, cur, nxt, prev, cur, nxt,
                  bias_first, bias_mid, bias_last],
        out_specs=qspec,
        out_shape=jax.ShapeDtypeStruct((batch, seq_len, cq), BF16),
        compiler_params=pltpu.CompilerParams(dimension_semantics=("parallel", "parallel"),
                                             vmem_limit_bytes=VMEM_LIMIT_BYTES),
        name="attn",
    )(sink, q3, k3, k3, k3, v3, v3, v3, bias_tab, bias_tab, bias_tab)
    return out.reshape(batch * seq_len, cq)


def _hgrn_consts(reverse):
    c, sub = HG_CHUNK, HG_SUB
    n_sub = c // sub
    t = np.arange(c)[:, None]
    s = np.arange(c)[None, :]
    same = (t // sub) == (s // sub)
    upto = (s >= t) if reverse else (s <= t)
    coeff = np.tile((same & upto).astype(np.float32), (1, 2))
    order = list(range(n_sub))[::-1] if reverse else list(range(n_sub))
    cols = [np.broadcast_to((t // sub) == order[a], (c, c)) for a in range(1, n_sub)]
    cols.append(same & upto)
    rmask = np.tile(np.concatenate(cols, axis=1), (HG_SLAB_HEADS, 1)).astype(np.float32)
    return coeff, rmask


def _hgrn_gates(fpre, lb):
    f = lb + (1.0 - lb) / (1.0 + jnp.exp2(fpre * (-LOG2E)))
    g = jnp.log2(f)
    g_hi = g.astype(BF16)
    g_lo = (g - g_hi.astype(F32)).astype(BF16)
    return g_hi, g_lo, 1.0 - f


def _hgrn_diag_exact(q, kk, b_rel, v, ones_bd, reverse):
    c, sub = HG_CHUNK, HG_SUB
    hk = q.shape[1]
    pos = lax.broadcasted_iota(jnp.int32, (c, hk), 0) % sub
    v32 = v.astype(F32)
    acc = jnp.zeros((c, hk), F32)
    for d in range(sub):
        shift = (c - d) % c if reverse else d
        if d == 0:
            k_s, b_s, v_s = kk, b_rel, v32
        else:
            k_s, b_s, v_s = (pltpu.roll(a, shift, 0) for a in (kk, b_rel, v32))
        valid = (pos <= sub - 1 - d) if reverse else (pos >= d)
        e = jnp.exp2(jnp.where(valid, b_rel - b_s, 0.0))
        p = jnp.where(valid, q * k_s * e, 0.0).astype(BF16)
        a = jnp.concatenate([_dot(p[:, j * HG_SLAB:(j + 1) * HG_SLAB], ones_bd)
                             for j in range(hk // HG_SLAB)], axis=1)
        acc = acc + a * v_s
    return acc


def _hgrn_block(q, fpre, v, lb, coeff, rmask, st_ref, reverse):
    c, sub = HG_CHUNK, HG_SUB
    n_sub = c // sub
    n_chunks = q.shape[0] // c
    hk = q.shape[1]
    n_slab = hk // HG_SLAB
    chunks = list(range(n_chunks))[::-1] if reverse else list(range(n_chunks))
    slabs = [slice(j * HG_SLAB, (j + 1) * HG_SLAB) for j in range(n_slab)]
    order = list(range(n_sub))[::-1] if reverse else list(range(n_sub))
    rows = lambda m: slice(m * sub, (m + 1) * sub)
    crow = lambda ci: slice(ci * c, (ci + 1) * c)
    low_half = lax.broadcasted_iota(jnp.int32, (c, LANES), 1) < HG_DIM
    zeros_sub = jnp.zeros((sub, hk), F32)
    zeros_tile = jnp.zeros((c, LANES), BF16)

    def head_tiles(x, j):
        return [(x[:, j * HG_SLAB + (h // 2) * LANES: j * HG_SLAB + (h // 2 + 1) * LANES], h % 2 == 0)
                for h in range(HG_SLAB_HEADS)]

    def keep_half(tile, low):
        return jnp.where(low_half, tile, 0.0) if low else jnp.where(low_half, 0.0, tile)

    def block_diag(tiles):
        blocks = [jnp.concatenate([t, zeros_tile] if h < 2 else [zeros_tile, t], axis=1)
                  for h, t in enumerate(tiles)]
        return jnp.concatenate(blocks, axis=0)

    g_hi, g_lo, kk = _hgrn_gates(fpre, lb)
    flags = []

    def running_sums(n, after=None):
        cf = coeff
        if after is not None:
            bits = pltpu.bitcast(after[0:8, 0:LANES], jnp.uint32)
            zero = pltpu.bitcast(lax.shift_right_logical(lax.shift_right_logical(bits, jnp.uint32(16)), jnp.uint32(16)), F32)
            cf = coeff + jnp.concatenate([zero] * (c // 8), axis=0)
        ci = chunks[n]
        return _dot(cf.astype(BF16), jnp.concatenate([g_hi[crow(ci)], g_lo[crow(ci)]], axis=0))

    def prepare(n, b_rel):
        ci = chunks[n]
        t_row = []
        for a in range(n_sub):
            last = order[a] * sub + (0 if reverse else sub - 1)
            t_row.append(b_rel[last:last + 1])
        e_row = [jnp.exp2(t) for t in t_row]
        q_rel = q[crow(ci)] * jnp.exp2(b_rel)
        unsafe = jnp.min(b_rel, axis=(0, 1), keepdims=True) < -HG_SAFE_LOG2
        flags.append(unsafe)
        k_diag = jnp.where(unsafe, 0.0, kk[crow(ci)] * jnp.exp2(-b_rel))
        k_rest = [None] * n_sub
        for a in range(n_sub):
            m = order[a]
            k_rest[m] = kk[crow(ci)][rows(m)] * jnp.exp2(t_row[a] - b_rel[rows(m)])
        p_out, acc = [None] * n_sub, None
        for a in range(n_sub):
            p_out[a] = acc
            acc = e_row[a] if acc is None else acc * e_row[a]
        gamma = acc
        p_in, acc = [None] * n_sub, None
        for a in reversed(range(n_sub)):
            p_in[a] = acc
            acc = e_row[a] if acc is None else acc * e_row[a]
        q_in, k_end = [None] * n_sub, [None] * n_sub
        for a in range(n_sub):
            m = order[a]
            q_in[m] = q_rel[rows(m)] if p_out[a] is None else q_rel[rows(m)] * p_out[a]
            k_end[m] = k_rest[m] if p_in[a] is None else k_rest[m] * p_in[a]
        variants = []
        for a1 in range(1, n_sub):
            parts, fac = [zeros_sub] * n_sub, None
            for a0 in range(a1 - 1, -1, -1):
                kr = k_rest[order[a0]]
                parts[order[a0]] = kr if fac is None else kr * fac
                fac = e_row[a0] if fac is None else fac * e_row[a0]
            variants.append(jnp.concatenate(parts, axis=0))
        variants.append(k_diag)
        return dict(
            q_rel=q_rel, gamma=gamma,
            q_in=jnp.concatenate(q_in, axis=0).astype(BF16),
            k_end=jnp.concatenate(k_end, axis=0).astype(BF16),
            k_stack=jnp.concatenate(variants, axis=0).astype(BF16))

    def score(p, j):
        lhs = block_diag([keep_half(t, low).astype(BF16) for t, low in head_tiles(p["q_rel"], j)])
        return (_dot_tb(lhs, p["k_stack"][:, slabs[j]]) * rmask).astype(BF16)

    def values(n):
        ci = chunks[n]
        o_all = [_dot(scores[n][j], jnp.concatenate([v[crow(ci), ls]] * n_sub, axis=0))
                 for j, ls in enumerate(slabs)]
        incr = [_dot_ta(v[crow(ci), ls], prep[n]["k_end"][:, ls]) for ls in slabs]
        return o_all, incr

    st = [[st_ref[j * HG_SLAB_HEADS + h] for h in range(HG_SLAB_HEADS)] for j in range(n_slab)]

    def finish(n):
        o_all, incr = vals[n]
        o_slabs = []
        for j, ls in enumerate(slabs):
            o_inter = _dot_tb(prep[n]["q_in"][:, ls], block_diag([t.astype(BF16) for t in st[j]]))
            gam = head_tiles(prep[n]["gamma"], j)
            for h in range(HG_SLAB_HEADS):
                tile = incr[j][h * HG_DIM:(h + 1) * HG_DIM, (h // 2) * LANES:(h // 2 + 1) * LANES]
                st[j][h] = st[j][h] * gam[h][0] + keep_half(tile, h % 2 == 0)
            oa = o_all[j]
            o_intra = jnp.concatenate(
                [jnp.where(low_half, oa[0:c, :LANES], oa[c:2 * c, :LANES]),
                 jnp.where(low_half, oa[2 * c:3 * c, LANES:], oa[3 * c:4 * c, LANES:])], axis=1)
            o_slabs.append(o_intra + o_inter)
        return jnp.concatenate(o_slabs, axis=1)

    prep, scores, vals = [None] * n_chunks, [None] * n_chunks, [None] * n_chunks
    outs = [None] * n_chunks
    n_stage = 4
    for it in range(n_chunks + n_stage - 1):
        if it < n_chunks:
            gate = vals[it - 3][0][0] if it >= 3 else None
            prep[it] = prepare(it, running_sums(it, gate))
        if 0 <= it - 1 < n_chunks:
            scores[it - 1] = [score(prep[it - 1], j) for j in range(n_slab)]
        if 0 <= it - 2 < n_chunks:
            vals[it - 2] = values(it - 2)
        if 0 <= it - 3 < n_chunks:
            outs[chunks[it - 3]] = finish(it - 3)
    for j in range(n_slab):
        for h in range(HG_SLAB_HEADS):
            st_ref[j * HG_SLAB_HEADS + h] = st[j][h]
    any_unsafe = functools.reduce(jnp.maximum, [fl.astype(jnp.int32) for fl in flags])
    return jnp.concatenate(outs, axis=0), any_unsafe


def _hgrn_add_exact_diag(o_ref, q_ref, f_ref, v_ref, lb, coeff, ones_bd, reverse):
    c = HG_CHUNK

    def body(ci, carry):
        r = pl.ds(pl.multiple_of(ci * c, c), c)
        g_hi, g_lo, kk = _hgrn_gates(f_ref[r, :], lb)
        b_rel = _dot(coeff.astype(BF16), jnp.concatenate([g_hi, g_lo], axis=0))
        unsafe = jnp.min(b_rel, axis=(0, 1), keepdims=True) < -HG_SAFE_LOG2
        extra = _hgrn_diag_exact(q_ref[r, :], kk, b_rel, v_ref[r, :], ones_bd, reverse)
        o_ref[r, :] = o_ref[r, :] + jnp.where(unsafe, extra, 0.0)
        return carry

    lax.fori_loop(0, q_ref.shape[0] // c, body, 0)


def _hgrn_fwd_kernel(q_ref, f_ref, v_ref, lb_ref, coeff_ref, rmask_ref, ones_ref, o_ref, st_ref):
    @pl.when(pl.program_id(1) == 0)
    def _():
        st_ref[...] = jnp.zeros_like(st_ref)

    o, any_unsafe = _hgrn_block(q_ref[...], f_ref[...], v_ref[...], lb_ref[...], coeff_ref[...],
                                rmask_ref[...], st_ref, False)
    o_ref[...] = o

    @pl.when(any_unsafe[0, 0] > 0)
    def _():
        _hgrn_add_exact_diag(o_ref, q_ref, f_ref, v_ref, lb_ref[...], coeff_ref[...], ones_ref[...], False)


def _hgrn_bwd_kernel(q_ref, f_ref, v_ref, lb_ref, coeff_ref, rmask_ref, of_ref, gate_ref, ng_ref, ones_ref,
                     o_ref, st_ref, acc_ref):
    @pl.when(pl.program_id(1) == 0)
    def _():
        st_ref[...] = jnp.zeros_like(st_ref)

    o_b, any_unsafe = _hgrn_block(q_ref[...], f_ref[...], v_ref[...], lb_ref[...], coeff_ref[...],
                                  rmask_ref[...], st_ref, True)
    acc_ref[...] = of_ref[...] + o_b
    ones_bd = ones_ref[...]

    @pl.when(any_unsafe[0, 0] > 0)
    def _():
        _hgrn_add_exact_diag(acc_ref, q_ref, f_ref, v_ref, lb_ref[...], coeff_ref[...], ones_bd, True)

    o = acc_ref[...]
    o2 = (o * o).astype(BF16)
    ms = jnp.concatenate([_dot(o2[:, j * HG_SLAB:(j + 1) * HG_SLAB], ones_bd)
                          for j in range(o.shape[1] // HG_SLAB)], axis=1) * (1.0 / HG_DIM)
    on = o * lax.rsqrt(ms + RMS_EPS) * ng_ref[...]
    o_ref[...] = (on * gate_ref[...].astype(F32)).astype(BF16)


def _hgrn(hq, hf_f, hf_b, hi, hg_gate, lower, norm_g, batch, seq_len):
    hk = hq.shape[-1]
    tb = HG_BLOCK
    nblk = seq_len // tb
    r3 = lambda a: a.reshape(batch, seq_len, a.shape[-1])
    const = lambda a: pl.BlockSpec(a.shape, lambda b, j: (0,) * a.ndim)
    fwd_blk = pl.BlockSpec((None, tb, hk), lambda b, j: (b, j, 0))
    bwd_blk = pl.BlockSpec((None, tb, hk), lambda b, j: (b, nblk - 1 - j, 0))
    cparams = pltpu.CompilerParams(dimension_semantics=("parallel", "arbitrary"),
                                   vmem_limit_bytes=VMEM_LIMIT_BYTES)
    state = pltpu.VMEM((hk // HG_DIM, HG_DIM, LANES), F32)
    lb_f, lb_b = lower[0:1], lower[1:2]
    consts_f = [jnp.asarray(a, F32) for a in _hgrn_consts(False)]
    consts_b = [jnp.asarray(a, F32) for a in _hgrn_consts(True)]
    head_of = np.arange(HG_SLAB) // HG_DIM
    ones_bd = jnp.asarray(head_of[:, None] == head_of[None, :], BF16)
    o_f = pl.pallas_call(
        _hgrn_fwd_kernel,
        grid=(batch, nblk),
        in_specs=[fwd_blk, fwd_blk, fwd_blk, const(lb_f)] + [const(a) for a in consts_f] + [const(ones_bd)],
        out_specs=fwd_blk,
        out_shape=jax.ShapeDtypeStruct((batch, seq_len, hk), F32),
        scratch_shapes=[state],
        compiler_params=cparams,
        name="hgrn_fwd",
    )(r3(hq), r3(hf_f), r3(hi), lb_f, *consts_f, ones_bd)
    ng = jnp.tile(norm_g.astype(F32), HG_HEADS)[None, :]
    o = pl.pallas_call(
        _hgrn_bwd_kernel,
        grid=(batch, nblk),
        in_specs=[bwd_blk, bwd_blk, bwd_blk, const(lb_b)] + [const(a) for a in consts_b]
                 + [bwd_blk, bwd_blk, const(ng), const(ones_bd)],
        out_specs=bwd_blk,
        out_shape=jax.ShapeDtypeStruct((batch, seq_len, hk), BF16),
        scratch_shapes=[state, pltpu.VMEM((tb, hk), F32)],
        compiler_params=cparams,
        name="hgrn_bwd",
    )(r3(hq), r3(hf_b), r3(hi), lb_b, *consts_b, o_f, r3(hg_gate), ng, ones_bd)
    return o.reshape(batch * seq_len, hk)


def _layer_norm(y, g, b):
    mu = jnp.mean(y, axis=-1, keepdims=True)
    d = y - mu
    var = jnp.mean(d * d, axis=-1, keepdims=True)
    return d * lax.rsqrt(var + LN_EPS) * g + b


def _mix_ffn_kernel(x_ref, oa_ref, oh_ref, ga_ref, gb_ref, wpa_ref, wph_ref, wout_ref, w1_ref, w2_ref,
                    g1_ref, b1_ref, g2_ref, b2_ref, y_ref):
    mixed = (ga_ref[...].astype(F32) * _dot(oa_ref[...], wpa_ref[...])
             + gb_ref[...].astype(F32) * _dot(oh_ref[...], wph_ref[...]))
    x1 = _layer_norm(ALPHA * x_ref[...] + _dot(mixed.astype(BF16), wout_ref[...]), g1_ref[...], b1_ref[...])
    h = jnp.maximum(_dot(x1.astype(BF16), w1_ref[...]), 0.0)
    ff = _dot((h * h).astype(BF16), w2_ref[...])
    y_ref[...] = _layer_norm(ALPHA * x1 + ff, g2_ref[...], b2_ref[...])


def _mix_ffn(x2d, o_att, o_hg, ga, gb, wpa, wph, wout, w1, w2, g1, b1, g2, b2):
    n_tok, d_model = x2d.shape
    tm = FFN_ROWS
    row = lambda c: pl.BlockSpec((tm, c), lambda i: (i, 0))
    resident = lambda a: pl.BlockSpec(a.shape, lambda i: (0, 0), pipeline_mode=pl.Buffered(1))
    vec = lambda a: pl.BlockSpec((1, a.shape[-1]), lambda i: (0, 0))
    v2 = lambda a: a.reshape(1, -1).astype(F32)
    return pl.pallas_call(
        _mix_ffn_kernel,
        grid=(n_tok // tm,),
        in_specs=[row(d_model), row(o_att.shape[1]), row(o_hg.shape[1]), row(d_model), row(d_model),
                  resident(wpa), resident(wph), resident(wout), resident(w1), resident(w2),
                  vec(g1), vec(b1), vec(g2), vec(b2)],
        out_specs=row(d_model),
        out_shape=jax.ShapeDtypeStruct((n_tok, d_model), F32),
        compiler_params=pltpu.CompilerParams(dimension_semantics=("parallel",),
                                             vmem_limit_bytes=VMEM_LIMIT_BYTES),
        name="mix_ffn",
    )(x2d, o_att, o_hg, ga, gb, wpa, wph, wout, w1, w2, v2(g1), v2(b1), v2(g2), v2(b2))


def _rope_tables(seq_len):
    half = ROPE_DIM // 2
    inv = ROPE_THETA ** (-jnp.arange(0, ROPE_DIM, 2, dtype=F32) / ROPE_DIM)
    ang = jnp.arange(seq_len, dtype=F32)[:, None] * inv[None, :]
    cos, sin = jnp.cos(ang), jnp.sin(ang)
    ones = jnp.ones((seq_len, HEAD_DIM - ROPE_DIM), F32)
    zeros_rest = jnp.zeros((seq_len, HEAD_DIM - ROPE_DIM), F32)
    zeros_half = jnp.zeros((seq_len, half), F32)
    c = jnp.concatenate([cos, cos, ones], axis=1)
    s1 = jnp.concatenate([zeros_half, sin, zeros_rest], axis=1)
    s2 = jnp.concatenate([-sin, zeros_half, zeros_rest], axis=1)
    rep = LANES // HEAD_DIM
    return tuple(jnp.tile(t, (1, rep)) for t in (c, s1, s2))


def _trunk(x, params, lower, rope_tabs):
    batch, seq_len, d_model = x.shape
    x2d = x.reshape(batch * seq_len, d_model)
    for l in range(DEPTH):
        p = params[l]
        q, k2, v2, hq, hf_f, hf_b, hi, hg_gate, ga, gb = _proj(x2d, p["w_in"], rope_tabs, seq_len, d_model)
        o_att = _attn(q, k2, v2, p["sink"], batch, seq_len)
        o_hg = _hgrn(hq, hf_f, hf_b, hi, hg_gate, lower[l], p["norm_g"], batch, seq_len)
        x2d = _mix_ffn(x2d, o_att, o_hg, ga, gb, p["wpa"], p["wph"], p["wout"], p["w1"], p["w2"],
                       p["g1"], p["b1"], p["g2"], p["b2"])
    return x2d.reshape(batch, seq_len, d_model)


def kernel(x_prompt, x_sample, w_in, att_sink, hgrn_lb, hgrn_norm_g, w_proj_att, w_proj_hgrn, w_out,
           ln1_g, ln1_b, w_ff1, w_ff2, ln2_g, ln2_b):
    sm = jax.nn.softmax(hgrn_lb.astype(F32), axis=0)
    lower = jnp.cumsum(sm, axis=0) - sm[0:1]
    params = []
    for l in range(DEPTH):
        params.append(dict(
            w_in=w_in[l].astype(BF16), sink=att_sink[l].astype(F32), norm_g=hgrn_norm_g[l],
            wpa=w_proj_att[l].astype(BF16), wph=w_proj_hgrn[l].astype(BF16), wout=w_out[l].astype(BF16),
            w1=w_ff1[l].astype(BF16), w2=w_ff2[l].astype(BF16),
            g1=ln1_g[l], b1=ln1_b[l], g2=ln2_g[l], b2=ln2_b[l]))
    rope_tabs = _rope_tables(x_prompt.shape[1])
    y_prompt = _trunk(x_prompt, params, lower, rope_tabs)
    if x_sample.shape[1] != x_prompt.shape[1]:
        rope_tabs = _rope_tables(x_sample.shape[1])
    y_sample = _trunk(x_sample, params, lower, rope_tabs)
    return (y_prompt, y_sample)
```

```python
import functools

import jax
import jax.numpy as jnp
import numpy as np
from jax import lax
from jax.experimental import pallas as pl
from jax.experimental.pallas import tpu as pltpu

F32 = jnp.float32
BF16 = jnp.bfloat16

ATT_HEADS = 8
ATT_KV_HEADS = 2
HEAD_DIM = 64
ATT_BLOCK = 128
ROPE_THETA = 500000.0
ROPE_DIM = HEAD_DIM // 4
HG_HEADS = 8
HG_DIM = 64
DEPTH = 2
ALPHA = (2 * DEPTH) ** 0.25
LN_EPS = 1e-5
RMS_EPS = 1e-6
NEG_BIG = -1e30
LOG2E = 1.4426950408889634

LANES = 128
MXU_DIM = 256
VMEM_LIMIT_BYTES = 56 * 1024 * 1024

PROJ_ROWS = 512
FFN_ROWS = 512
ATT_TILE = 512
HG_BLOCK = 1024
HG_CHUNK = 64
HG_SUB = 16
HG_SLAB = MXU_DIM
HG_SLAB_HEADS = HG_SLAB // HG_DIM
HG_SAFE_LOG2 = 96.0


def _sigmoid(x):
    return 1.0 / (1.0 + jnp.exp(-x))


def _dot(a, b):
    return jnp.dot(a, b, preferred_element_type=F32)


def _dot_tb(a, b):
    return lax.dot_general(a, b, (((1,), (1,)), ((), ())), preferred_element_type=F32)


def _dot_ta(a, b):
    return lax.dot_general(a, b, (((0,), (0,)), ((), ())), preferred_element_type=F32)


def _proj_kernel(x_ref, w_ref, rc_ref, rs1_ref, rs2_ref,
                 q_ref, k_ref, v_ref, hq_ref, ff_ref, fb_ref, hi_ref, hg_ref, ga_ref, gb_ref,
                 *, cuts):
    xb = x_ref[...].astype(BF16)

    def mm(name):
        a, b = cuts[name]
        return _dot(xb, w_ref[:, a:b])

    rc = rc_ref[...]
    rs1 = rs1_ref[...]
    rs2 = rs2_ref[...]

    def rope(blk):
        return blk * rc + pltpu.roll(blk, ROPE_DIM // 2, 1) * rs1 + pltpu.roll(blk, LANES - ROPE_DIM // 2, 1) * rs2

    aq = mm("aq")
    for j in range(aq.shape[1] // LANES):
        blk = rope(aq[:, j * LANES:(j + 1) * LANES]) * (LOG2E * HEAD_DIM ** -0.5)
        q_ref[:, j * LANES:(j + 1) * LANES] = blk.astype(BF16)
    akv = mm("akv")
    ak = rope(akv[:, :LANES])
    k_ref[:, :LANES] = ak.astype(BF16)
    k_ref[:, LANES:] = pltpu.roll(ak, HEAD_DIM, 1).astype(BF16)
    av = akv[:, LANES:]
    v_ref[:, :LANES] = av.astype(BF16)
    v_ref[:, LANES:] = pltpu.roll(av, HEAD_DIM, 1).astype(BF16)

    ga_ref[...] = _sigmoid(mm("ga")).astype(BF16)
    gb_ref[...] = _sigmoid(mm("gb")).astype(BF16)
    hq = mm("hq")
    hq_ref[...] = hq * _sigmoid(hq) * (HG_DIM ** -0.5)
    hg = mm("hg")
    hg_ref[...] = (hg * _sigmoid(hg)).astype(BF16)
    hi_ref[...] = mm("hi").astype(BF16)
    ff_ref[...] = mm("hf_f")
    fb_ref[...] = mm("hf_b")


def _proj(x2d, w_in, rope_tabs, seq_len, d_model):
    n_tok = x2d.shape[0]
    tm = PROJ_ROWS
    att_q = ATT_HEADS * HEAD_DIM
    att_kv = ATT_KV_HEADS * HEAD_DIM
    hg = HG_HEADS * HG_DIM
    sizes = [("aq", att_q), ("akv", 2 * att_kv), ("hq", hg), ("hf_f", hg), ("hf_b", hg),
             ("hi", hg), ("hg", hg), ("ga", d_model), ("gb", d_model)]
    cuts, off = {}, 0
    for name, s in sizes:
        cuts[name] = (off, off + s)
        off += s
    d_in = off
    pos_blocks = seq_len // tm
    row = lambda c: pl.BlockSpec((tm, c), lambda i: (i, 0))
    tab = pl.BlockSpec((tm, LANES), lambda i: (i % pos_blocks, 0))
    out_shapes = [
        jax.ShapeDtypeStruct((n_tok, att_q), BF16),
        jax.ShapeDtypeStruct((n_tok, 2 * att_kv), BF16),
        jax.ShapeDtypeStruct((n_tok, 2 * att_kv), BF16),
        jax.ShapeDtypeStruct((n_tok, hg), F32),
        jax.ShapeDtypeStruct((n_tok, hg), F32),
        jax.ShapeDtypeStruct((n_tok, hg), F32),
        jax.ShapeDtypeStruct((n_tok, hg), BF16),
        jax.ShapeDtypeStruct((n_tok, hg), BF16),
        jax.ShapeDtypeStruct((n_tok, d_model), BF16),
        jax.ShapeDtypeStruct((n_tok, d_model), BF16),
    ]
    return pl.pallas_call(
        functools.partial(_proj_kernel, cuts=cuts),
        grid=(n_tok // tm,),
        in_specs=[row(d_model),
                  pl.BlockSpec((d_model, d_in), lambda i: (0, 0), pipeline_mode=pl.Buffered(1)),
                  tab, tab, tab],
        out_specs=[row(s.shape[1]) for s in out_shapes],
        out_shape=out_shapes,
        compiler_params=pltpu.CompilerParams(dimension_semantics=("parallel",),
                                             vmem_limit_bytes=VMEM_LIMIT_BYTES),
        name="proj",
    )(x2d, w_in, *rope_tabs)


def _attn_bias_table():
    blk = ATT_BLOCK
    qi = np.arange(blk)[:, None]
    kj = np.arange(blk)[None, :]
    prev_ok = kj >= qi
    next_ok = kj <= qi
    none = np.zeros((blk, blk), bool)
    tab = []
    for p_ok, n_ok in ((prev_ok, next_ok), (none, next_ok), (prev_ok, none)):
        m = np.concatenate([p_ok, n_ok], axis=1)
        tab.append(np.where(np.concatenate([m, m], axis=0), 0.0, NEG_BIG))
    return np.stack(tab).astype(np.float32)


def _attn_kernel(sink_ref, q_ref, kp_ref, kc_ref, kn_ref, vp_ref, vc_ref, vn_ref,
                 bias_first_ref, bias_mid_ref, bias_last_ref, o_ref):
    blk = ATT_BLOCK
    n_qb = q_ref.shape[0] // blk
    k = jnp.concatenate([kp_ref[...], kc_ref[...], kn_ref[...]], axis=0)
    v = jnp.concatenate([vp_ref[...], vc_ref[...], vn_ref[...]], axis=0)
    rows_kv = k.shape[0]
    lo = lax.broadcasted_iota(jnp.int32, (rows_kv, LANES), 1) < HEAD_DIM
    zero = jnp.zeros((rows_kv, LANES), BF16)
    k_arr, k_rot = k[:, :LANES], k[:, LANES:]
    v_arr, v_rot = v[:, :LANES], v[:, LANES:]
    k_even = [jnp.where(lo, k_arr, zero), jnp.where(lo, k_rot, zero)]
    k_odd = [jnp.where(lo, zero, k_rot), jnp.where(lo, zero, k_arr)]
    v_even = [jnp.where(lo, v_arr, zero), jnp.where(lo, v_rot, zero)]
    v_odd = [jnp.where(lo, zero, v_rot), jnp.where(lo, zero, v_arr)]
    top = lax.broadcasted_iota(jnp.int32, (2 * blk, 1), 0) < blk
    lo_out = lax.broadcasted_iota(jnp.int32, (2 * blk, LANES), 1) < HEAD_DIM

    def softmax_parts(s, bias, sink_col):
        s_prev = s[:, :blk] + bias[:, :blk]
        s_mid = s[:, blk:2 * blk]
        s_next = s[:, 2 * blk:] + bias[:, blk:]
        m = jnp.max(jnp.maximum(jnp.maximum(s_prev, s_mid), s_next), axis=-1, keepdims=True)
        m = jnp.maximum(m, sink_col)
        p_prev, p_mid, p_next = jnp.exp2(s_prev - m), jnp.exp2(s_mid - m), jnp.exp2(s_next - m)
        den = jnp.sum(p_prev + p_mid + p_next, axis=-1, keepdims=True) + jnp.exp2(sink_col - m)
        return jnp.concatenate([p_prev, p_mid, p_next], axis=1).astype(BF16), 1.0 / den

    n_slab = ATT_HEADS * HEAD_DIM // LANES
    per_kv = n_slab // ATT_KV_HEADS
    for t in range(n_qb):
        bias = (bias_first_ref if t == 0 else bias_last_ref if t == n_qb - 1 else bias_mid_ref)[...]
        qr = slice(t * blk, (t + 1) * blk)
        kr = slice(t * blk, (t + 3) * blk)
        for g in range(ATT_KV_HEADS):
            slabs = [q_ref[qr, (per_kv * g + u) * LANES:(per_kv * g + u + 1) * LANES] for u in range(per_kv)]
            qs = jnp.concatenate(slabs, axis=0)
            h0 = 2 * per_kv * g
            sink_e = jnp.where(top, sink_ref[h0], sink_ref[h0 + 2]) * LOG2E
            sink_o = jnp.where(top, sink_ref[h0 + 1], sink_ref[h0 + 3]) * LOG2E
            p_e, r_e = softmax_parts(_dot_tb(qs, k_even[g][kr]), bias, sink_e)
            p_o, r_o = softmax_parts(_dot_tb(qs, k_odd[g][kr]), bias, sink_o)
            pv = _dot(jnp.concatenate([p_e, p_o], axis=1),
                      jnp.concatenate([v_even[g][kr], v_odd[g][kr]], axis=0))
            o = pv * jnp.where(lo_out, r_e, r_o)
            for u in range(per_kv):
                o_ref[qr, (per_kv * g + u) * LANES:(per_kv * g + u + 1) * LANES] = \
                    o[u * blk:(u + 1) * blk].astype(BF16)


def _attn(q, k2, v2, sink, batch, seq_len):
    blk = ATT_BLOCK
    tq = ATT_TILE
    per = tq // blk
    nt = seq_len // tq
    nb = seq_len // blk
    assert nb >= 2
    cq = q.shape[-1]
    ck = k2.shape[-1]
    q3 = q.reshape(batch, seq_len, cq)
    k3 = k2.reshape(batch, seq_len, ck)
    v3 = v2.reshape(batch, seq_len, ck)
    prev = pl.BlockSpec((None, blk, ck), lambda b, i: (b, jnp.maximum(i * per - 1, 0), 0))
    cur = pl.BlockSpec((None, tq, ck), lambda b, i: (b, i, 0))
    nxt = pl.BlockSpec((None, blk, ck), lambda b, i: (b, jnp.minimum((i + 1) * per, nb - 1), 0))
    qspec = pl.BlockSpec((None, tq, cq), lambda b, i: (b, i, 0))
    bias_tab = jnp.asarray(_attn_bias_table())
    bshape = (None,) + bias_tab.shape[1:]
    bias_first = pl.BlockSpec(bshape, lambda b, i: (jnp.where(i == 0, 1, 0), 0, 0))
    bias_mid = pl.BlockSpec(bshape, lambda b, i: (0, 0, 0))
    bias_last = pl.BlockSpec(bshape, lambda b, i: (jnp.where(i == nt - 1, 2, 0), 0, 0))
    out = pl.pallas_call(
        _attn_kernel,
        grid=(batch, nt),
        in_specs=[pl.BlockSpec(memory_space=pltpu.SMEM), qspec, prev, cur, nxt, prev, cur, nxt,
                  bias_first, bias_mid, bias_last],
        out_specs=qspec,
        out_shape=jax.ShapeDtypeStruct((batch, seq_len, cq), BF16),
        compiler_params=pltpu.CompilerParams(dimension_semantics=("parallel", "parallel"),
                                             vmem_limit_bytes=VMEM_LIMIT_BYTES),
        name="attn",
    )(sink, q3, k3, k3, k3, v3, v3, v3, bias_tab, bias_tab, bias_tab)
    return out.reshape(batch * seq_len, cq)


def _hgrn_consts(reverse):
    c, sub = HG_CHUNK, HG_SUB
    n_sub = c // sub
    t = np.arange(c)[:, None]
    s = np.arange(c)[None, :]
    same = (t // sub) == (s // sub)
    upto = (s >= t) if reverse else (s <= t)
    coeff = np.tile((same & upto).astype(np.float32), (1, 2))
    order = list(range(n_sub))[::-1] if reverse else list(range(n_sub))
    cols = [np.broadcast_to((t // sub) == order[a], (c, c)) for a in range(1, n_sub)]
    cols.append(same & upto)
    rmask = np.tile(np.concatenate(cols, axis=1), (HG_SLAB_HEADS, 1)).astype(np.float32)
    return coeff, rmask


def _hgrn_gates(fpre, lb):
    f = lb + (1.0 - lb) / (1.0 + jnp.exp2(fpre * (-LOG2E)))
    g = jnp.log2(f)
    g_hi = g.astype(BF16)
    g_lo = (g - g_hi.astype(F32)).astype(BF16)
    return g_hi, g_lo, 1.0 - f


def _hgrn_diag_exact(q, kk, b_rel, v, ones_bd, reverse):
    c, sub = HG_CHUNK, HG_SUB
    hk = q.shape[1]
    pos = lax.broadcasted_iota(jnp.int32, (c, hk), 0) % sub
    v32 = v.astype(F32)
    acc = jnp.zeros((c, hk), F32)
    for d in range(sub):
        shift = (c - d) % c if reverse else d
        if d == 0:
            k_s, b_s, v_s = kk, b_rel, v32
        else:
            k_s, b_s, v_s = (pltpu.roll(a, shift, 0) for a in (kk, b_rel, v32))
        valid = (pos <= sub - 1 - d) if reverse else (pos >= d)
        e = jnp.exp2(jnp.where(valid, b_rel - b_s, 0.0))
        p = jnp.where(valid, q * k_s * e, 0.0).astype(BF16)
        a = jnp.concatenate([_dot(p[:, j * HG_SLAB:(j + 1) * HG_SLAB], ones_bd)
                             for j in range(hk // HG_SLAB)], axis=1)
        acc = acc + a * v_s
    return acc


def _hgrn_block(q, fpre, v, lb, coeff, rmask, st_ref, reverse):
    c, sub = HG_CHUNK, HG_SUB
    n_sub = c // sub
    n_chunks = q.shape[0] // c
    hk = q.shape[1]
    n_slab = hk // HG_SLAB
    chunks = list(range(n_chunks))[::-1] if reverse else list(range(n_chunks))
    slabs = [slice(j * HG_SLAB, (j + 1) * HG_SLAB) for j in range(n_slab)]
    order = list(range(n_sub))[::-1] if reverse else list(range(n_sub))
    rows = lambda m: slice(m * sub, (m + 1) * sub)
    crow = lambda ci: slice(ci * c, (ci + 1) * c)
    low_half = lax.broadcasted_iota(jnp.int32, (c, LANES), 1) < HG_DIM
    zeros_sub = jnp.zeros((sub, hk), F32)
    zeros_tile = jnp.zeros((c, LANES), BF16)

    def head_tiles(x, j):
        return [(x[:, j * HG_SLAB + (h // 2) * LANES: j * HG_SLAB + (h // 2 + 1) * LANES], h % 2 == 0)
                for h in range(HG_SLAB_HEADS)]

    def keep_half(tile, low):
        return jnp.where(low_half, tile, 0.0) if low else jnp.where(low_half, 0.0, tile)

    def block_diag(tiles):
        blocks = [jnp.concatenate([t, zeros_tile] if h < 2 else [zeros_tile, t], axis=1)
                  for h, t in enumerate(tiles)]
        return jnp.concatenate(blocks, axis=0)

    g_hi, g_lo, kk = _hgrn_gates(fpre, lb)
    flags = []

    def running_sums(n, after=None):
        cf = coeff
        if after is not None:
            bits = pltpu.bitcast(after[0:8, 0:LANES], jnp.uint32)
            zero = pltpu.bitcast(lax.shift_right_logical(lax.shift_right_logical(bits, jnp.uint32(16)), jnp.uint32(16)), F32)
            cf = coeff + jnp.concatenate([zero] * (c // 8), axis=0)
        ci = chunks[n]
        return _dot(cf.astype(BF16), jnp.concatenate([g_hi[crow(ci)], g_lo[crow(ci)]], axis=0))

    def prepare(n, b_rel):
        ci = chunks[n]
        t_row = []
        for a in range(n_sub):
            last = order[a] * sub + (0 if reverse else sub - 1)
            t_row.append(b_rel[last:last + 1])
        e_row = [jnp.exp2(t) for t in t_row]
        q_rel = q[crow(ci)] * jnp.exp2(b_rel)
        unsafe = jnp.min(b_rel, axis=(0, 1), keepdims=True) < -HG_SAFE_LOG2
        flags.append(unsafe)
        k_diag = jnp.where(unsafe, 0.0, kk[crow(ci)] * jnp.exp2(-b_rel))
        k_rest = [None] * n_sub
        for a in range(n_sub):
            m = order[a]
            k_rest[m] = kk[crow(ci)][rows(m)] * jnp.exp2(t_row[a] - b_rel[rows(m)])
        p_out, acc = [None] * n_sub, None
        for a in range(n_sub):
            p_out[a] = acc
            acc = e_row[a] if acc is None else acc * e_row[a]
        gamma = acc
        p_in, acc = [None] * n_sub, None
        for a in reversed(range(n_sub)):
            p_in[a] = acc
            acc = e_row[a] if acc is None else acc * e_row[a]
        q_in, k_end = [None] * n_sub, [None] * n_sub
        for a in range(n_sub):
            m = order[a]
            q_in[m] = q_rel[rows(m)] if p_out[a] is None else q_rel[rows(m)] * p_out[a]
            k_end[m] = k_rest[m] if p_in[a] is None else k_rest[m] * p_in[a]
        variants = []
        for a1 in range(1, n_sub):
            parts, fac = [zeros_sub] * n_sub, None
            for a0 in range(a1 - 1, -1, -1):
                kr = k_rest[order[a0]]
                parts[order[a0]] = kr if fac is None else kr * fac
                fac = e_row[a0] if fac is None else fac * e_row[a0]
            variants.append(jnp.concatenate(parts, axis=0))
        variants.append(k_diag)
        return dict(
            q_rel=q_rel, gamma=gamma,
            q_in=jnp.concatenate(q_in, axis=0).astype(BF16),
            k_end=jnp.concatenate(k_end, axis=0).astype(BF16),
            k_stack=jnp.concatenate(variants, axis=0).astype(BF16))

    def score(p, j):
        lhs = block_diag([keep_half(t, low).astype(BF16) for t, low in head_tiles(p["q_rel"], j)])
        return (_dot_tb(lhs, p["k_stack"][:, slabs[j]]) * rmask).astype(BF16)

    def values(n):
        ci = chunks[n]
        o_all = [_dot(scores[n][j], jnp.concatenate([v[crow(ci), ls]] * n_sub, axis=0))
                 for j, ls in enumerate(slabs)]
        incr = [_dot_ta(v[crow(ci), ls], prep[n]["k_end"][:, ls]) for ls in slabs]
        return o_all, incr

    st = [[st_ref[j * HG_SLAB_HEADS + h] for h in range(HG_SLAB_HEADS)] for j in range(n_slab)]

    def finish(n):
        o_all, incr = vals[n]
        o_slabs = []
        for j, ls in enumerate(slabs):
            o_inter = _dot_tb(prep[n]["q_in"][:, ls], block_diag([t.astype(BF16) for t in st[j]]))
            gam = head_tiles(prep[n]["gamma"], j)
            for h in range(HG_SLAB_HEADS):
                tile = incr[j][h * HG_DIM:(h + 1) * HG_DIM, (h // 2) * LANES:(h // 2 + 1) * LANES]
                st[j][h] = st[j][h] * gam[h][0] + keep_half(tile, h % 2 == 0)
            oa = o_all[j]
            o_intra = jnp.concatenate(
                [jnp.where(low_half, oa[0:c, :LANES], oa[c:2 * c, :LANES]),
                 jnp.where(low_half, oa[2 * c:3 * c, LANES:], oa[3 * c:4 * c, LANES:])], axis=1)
            o_slabs.append(o_intra + o_inter)
        return jnp.concatenate(o_slabs, axis=1)

    prep, scores, vals = [None] * n_chunks, [None] * n_chunks, [None] * n_chunks
    outs = [None] * n_chunks
    n_stage = 4
    for it in range(n_chunks + n_stage - 1):
        if it < n_chunks:
            gate = vals[it - 3][0][0] if it >= 3 else None
            prep[it] = prepare(it, running_sums(it, gate))
        if 0 <= it - 1 < n_chunks:
            scores[it - 1] = [score(prep[it - 1], j) for j in range(n_slab)]
        if 0 <= it - 2 < n_chunks:
            vals[it - 2] = values(it - 2)
        if 0 <= it - 3 < n_chunks:
            outs[chunks[it - 3]] = finish(it - 3)
    for j in range(n_slab):
        for h in range(HG_SLAB_HEADS):
            st_ref[j * HG_SLAB_HEADS + h] = st[j][h]
    any_unsafe = functools.reduce(jnp.maximum, [fl.astype(jnp.int32) for fl in flags])
    return jnp.concatenate(outs, axis=0), any_unsafe


def _hgrn_add_exact_diag(o_ref, q_ref, f_ref, v_ref, lb, coeff, ones_bd, reverse):
    c = HG_CHUNK

    def body(ci, carry):
        r = pl.ds(pl.multiple_of(ci * c, c), c)
        g_hi, g_lo, kk = _hgrn_gates(f_ref[r, :], lb)
        b_rel = _dot(coeff.astype(BF16), jnp.concatenate([g_hi, g_lo], axis=0))
        unsafe = jnp.min(b_rel, axis=(0, 1), keepdims=True) < -HG_SAFE_LOG2
        extra = _hgrn_diag_exact(q_ref[r, :], kk, b_rel, v_ref[r, :], ones_bd, reverse)
        o_ref[r, :] = o_ref[r, :] + jnp.where(unsafe, extra, 0.0)
        return carry

    lax.fori_loop(0, q_ref.shape[0] // c, body, 0)


def _hgrn_fwd_kernel(q_ref, f_ref, v_ref, lb_ref, coeff_ref, rmask_ref, ones_ref, o_ref, st_ref):
    @pl.when(pl.program_id(1) == 0)
    def _():
        st_ref[...] = jnp.zeros_like(st_ref)

    o, any_unsafe = _hgrn_block(q_ref[...], f_ref[...], v_ref[...], lb_ref[...], coeff_ref[...],
                                rmask_ref[...], st_ref, False)
    o_ref[...] = o

    @pl.when(any_unsafe[0, 0] > 0)
    def _():
        _hgrn_add_exact_diag(o_ref, q_ref, f_ref, v_ref, lb_ref[...], coeff_ref[...], ones_ref[...], False)


def _hgrn_bwd_kernel(q_ref, f_ref, v_ref, lb_ref, coeff_ref, rmask_ref, of_ref, gate_ref, ng_ref, ones_ref,
                     o_ref, st_ref, acc_ref):
    @pl.when(pl.program_id(1) == 0)
    def _():
        st_ref[...] = jnp.zeros_like(st_ref)

    o_b, any_unsafe = _hgrn_block(q_ref[...], f_ref[...], v_ref[...], lb_ref[...], coeff_ref[...],
                                  rmask_ref[...], st_ref, True)
    acc_ref[...] = of_ref[...] + o_b
    ones_bd = ones_ref[...]

    @pl.when(any_unsafe[0, 0] > 0)
    def _():
        _hgrn_add_exact_diag(acc_ref, q_ref, f_ref, v_ref, lb_ref[...], coeff_ref[...], ones_bd, True)

    o = acc_ref[...]
    o2 = (o * o).astype(BF16)
    ms = jnp.concatenate([_dot(o2[:, j * HG_SLAB:(j + 1) * HG_SLAB], ones_bd)
                          for j in range(o.shape[1] // HG_SLAB)], axis=1) * (1.0 / HG_DIM)
    on = o * lax.rsqrt(ms + RMS_EPS) * ng_ref[...]
    o_ref[...] = (on * gate_ref[...].astype(F32)).astype(BF16)


def _hgrn(hq, hf_f, hf_b, hi, hg_gate, lower, norm_g, batch, seq_len):
    hk = hq.shape[-1]
    tb = HG_BLOCK
    nblk = seq_len // tb
    r3 = lambda a: a.reshape(batch, seq_len, a.shape[-1])
    const = lambda a: pl.BlockSpec(a.shape, lambda b, j: (0,) * a.ndim)
    fwd_blk = pl.BlockSpec((None, tb, hk), lambda b, j: (b, j, 0))
    bwd_blk = pl.BlockSpec((None, tb, hk), lambda b, j: (b, nblk - 1 - j, 0))
    cparams = pltpu.CompilerParams(dimension_semantics=("parallel", "arbitrary"),
                                   vmem_limit_bytes=VMEM_LIMIT_BYTES)
    state = pltpu.VMEM((hk // HG_DIM, HG_DIM, LANES), F32)
    lb_f, lb_b = lower[0:1], lower[1:2]
    consts_f = [jnp.asarray(a, F32) for a in _hgrn_consts(False)]
    consts_b = [jnp.asarray(a, F32) for a in _hgrn_consts(True)]
    head_of = np.arange(HG_SLAB) // HG_DIM
    ones_bd = jnp.asarray(head_of[:, None] == head_of[None, :], BF16)
    o_f = pl.pallas_call(
        _hgrn_fwd_kernel,
        grid=(batch, nblk),
        in_specs=[fwd_blk, fwd_blk, fwd_blk, const(lb_f)] + [const(a) for a in consts_f] + [const(ones_bd)],
        out_specs=fwd_blk,
        out_shape=jax.ShapeDtypeStruct((batch, seq_len, hk), F32),
        scratch_shapes=[state],
        compiler_params=cparams,
        name="hgrn_fwd",
    )(r3(hq), r3(hf_f), r3(hi), lb_f, *consts_f, ones_bd)
    ng = jnp.tile(norm_g.astype(F32), HG_HEADS)[None, :]
    o = pl.pallas_call(
        _hgrn_bwd_kernel,
        grid=(batch, nblk),
        in_specs=[bwd_blk, bwd_blk, bwd_blk, const(lb_b)] + [const(a) for a in consts_b]
                 + [bwd_blk, bwd_blk, const(ng), const(ones_bd)],
        out_specs=bwd_blk,
        out_shape=jax.ShapeDtypeStruct((batch, seq_len, hk), BF16),
        scratch_shapes=[state, pltpu.VMEM((tb, hk), F32)],
        compiler_params=cparams,
        name="hgrn_bwd",
    )(r3(hq), r3(hf_b), r3(hi), lb_b, *consts_b, o_f, r3(hg_gate), ng, ones_bd)
    return o.reshape(batch * seq_len, hk)


def _layer_norm(y, g, b):
    mu = jnp.mean(y, axis=-1, keepdims=True)
    d = y - mu
    var = jnp.mean(d * d, axis=-1, keepdims=True)
    return d * lax.rsqrt(var + LN_EPS) * g + b


def _mix_ffn_kernel(x_ref, oa_ref, oh_ref, ga_ref, gb_ref, wpa_ref, wph_ref, wout_ref, w1_ref, w2_ref,
                    g1_ref, b1_ref, g2_ref, b2_ref, y_ref):
    tm = x_ref.shape[0]
    halves = [slice(0, tm // 2), slice(tm // 2, tm)]

    def merge(r):
        mixed = (ga_ref[r, :].astype(F32) * _dot(oa_ref[r, :], wpa_ref[...])
                 + gb_ref[r, :].astype(F32) * _dot(oh_ref[r, :], wph_ref[...]))
        return ALPHA * x_ref[r, :] + _dot(mixed.astype(BF16), wout_ref[...])

    def hidden(x1):
        h = jnp.maximum(_dot(x1.astype(BF16), w1_ref[...]), 0.0)
        return (h * h).astype(BF16)

    pre = [merge(r) for r in halves]
    x1, hh = [None, None], [None, None]
    for i in range(2):
        x1[i] = _layer_norm(pre[i], g1_ref[...], b1_ref[...])
        hh[i] = hidden(x1[i])
    z = [ALPHA * x1[i] + _dot(hh[i], w2_ref[...]) for i in range(2)]
    for i, r in enumerate(halves):
        y_ref[r, :] = _layer_norm(z[i], g2_ref[...], b2_ref[...])


def _mix_ffn(x2d, o_att, o_hg, ga, gb, wpa, wph, wout, w1, w2, g1, b1, g2, b2):
    n_tok, d_model = x2d.shape
    tm = FFN_ROWS
    row = lambda c: pl.BlockSpec((tm, c), lambda i: (i, 0))
    resident = lambda a: pl.BlockSpec(a.shape, lambda i: (0, 0), pipeline_mode=pl.Buffered(1))
    vec = lambda a: pl.BlockSpec((1, a.shape[-1]), lambda i: (0, 0))
    v2 = lambda a: a.reshape(1, -1).astype(F32)
    return pl.pallas_call(
        _mix_ffn_kernel,
        grid=(n_tok // tm,),
        in_specs=[row(d_model), row(o_att.shape[1]), row(o_hg.shape[1]), row(d_model), row(d_model),
                  resident(wpa), resident(wph), resident(wout), resident(w1), resident(w2),
                  vec(g1), vec(b1), vec(g2), vec(b2)],
        out_specs=row(d_model),
        out_shape=jax.ShapeDtypeStruct((n_tok, d_model), F32),
        compiler_params=pltpu.CompilerParams(dimension_semantics=("parallel",),
                                             vmem_limit_bytes=VMEM_LIMIT_BYTES),
        name="mix_ffn",
    )(x2d, o_att, o_hg, ga, gb, wpa, wph, wout, w1, w2, v2(g1), v2(b1), v2(g2), v2(b2))


def _rope_tables(seq_len):
    half = ROPE_DIM // 2
    inv = ROPE_THETA ** (-jnp.arange(0, ROPE_DIM, 2, dtype=F32) / ROPE_DIM)
    ang = jnp.arange(seq_len, dtype=F32)[:, None] * inv[None, :]
    cos, sin = jnp.cos(ang), jnp.sin(ang)
    ones = jnp.ones((seq_len, HEAD_DIM - ROPE_DIM), F32)
    zeros_rest = jnp.zeros((seq_len, HEAD_DIM - ROPE_DIM), F32)
    zeros_half = jnp.zeros((seq_len, half), F32)
    c = jnp.concatenate([cos, cos, ones], axis=1)
    s1 = jnp.concatenate([zeros_half, sin, zeros_rest], axis=1)
    s2 = jnp.concatenate([-sin, zeros_half, zeros_rest], axis=1)
    rep = LANES // HEAD_DIM
    return tuple(jnp.tile(t, (1, rep)) for t in (c, s1, s2))


def _trunk(x, params, lower, rope_tabs):
    batch, seq_len, d_model = x.shape
    x2d = x.reshape(batch * seq_len, d_model)
    for l in range(DEPTH):
        p = params[l]
        q, k2, v2, hq, hf_f, hf_b, hi, hg_gate, ga, gb = _proj(x2d, p["w_in"], rope_tabs, seq_len, d_model)
        o_att = _attn(q, k2, v2, p["sink"], batch, seq_len)
        o_hg = _hgrn(hq, hf_f, hf_b, hi, hg_gate, lower[l], p["norm_g"], batch, seq_len)
        x2d = _mix_ffn(x2d, o_att, o_hg, ga, gb, p["wpa"], p["wph"], p["wout"], p["w1"], p["w2"],
                       p["g1"], p["b1"], p["g2"], p["b2"])
    return x2d.reshape(batch, seq_len, d_model)


def kernel(x_prompt, x_sample, w_in, att_sink, hgrn_lb, hgrn_norm_g, w_proj_att, w_proj_hgrn, w_out,
           ln1_g, ln1_b, w_ff1, w_ff2, ln2_g, ln2_b):
    sm = jax.nn.softmax(hgrn_lb.astype(F32), axis=0)
    lower = jnp.cumsum(sm, axis=0) - sm[0:1]
    params = []
    for l in range(DEPTH):
        params.append(dict(
            w_in=w_in[l].astype(BF16), sink=att_sink[l].astype(F32), norm_g=hgrn_norm_g[l],
            wpa=w_proj_att[l].astype(BF16), wph=w_proj_hgrn[l].astype(BF16), wout=w_out[l].astype(BF16),
            w1=w_ff1[l].astype(BF16), w2=w_ff2[l].astype(BF16),
            g1=ln1_g[l], b1=ln1_b[l], g2=ln2_g[l], b2=ln2_b[l]))
    rope_tabs = _rope_tables(x_prompt.shape[1])
    y_prompt = _trunk(x_prompt, params, lower, rope_tabs)
    if x_sample.shape[1] != x_prompt.shape[1]:
        rope_tabs = _rope_tables(x_sample.shape[1])
    y_sample = _trunk(x_sample, params, lower, rope_tabs)
    return (y_prompt, y_sample)
```

```python
import functools

import jax
import jax.numpy as jnp
import numpy as np
from jax import lax
from jax.experimental import pallas as pl
from jax.experimental.pallas import tpu as pltpu

F32 = jnp.float32
BF16 = jnp.bfloat16

ATT_HEADS = 8
ATT_KV_HEADS = 2
HEAD_DIM = 64
ATT_BLOCK = 128
ROPE_THETA = 500000.0
ROPE_DIM = HEAD_DIM // 4
HG_HEADS = 8
HG_DIM = 64
DEPTH = 2
ALPHA = (2 * DEPTH) ** 0.25
LN_EPS = 1e-5
RMS_EPS = 1e-6
NEG_BIG = -1e30
LOG2E = 1.4426950408889634

LANES = 128
MXU_DIM = 256
VMEM_LIMIT_BYTES = 56 * 1024 * 1024

PROJ_ROWS = 512
FFN_ROWS = 512
ATT_TILE = 512
HG_BLOCK = 1024
HG_CHUNK = 64
HG_SUB = 16
HG_SLAB = MXU_DIM
HG_SLAB_HEADS = HG_SLAB // HG_DIM
HG_SAFE_LOG2 = 96.0


def _sigmoid(x):
    return 1.0 / (1.0 + jnp.exp(-x))


def _dot(a, b):
    return jnp.dot(a, b, preferred_element_type=F32)


def _dot_tb(a, b):
    return lax.dot_general(a, b, (((1,), (1,)), ((), ())), preferred_element_type=F32)


def _dot_ta(a, b):
    return lax.dot_general(a, b, (((0,), (0,)), ((), ())), preferred_element_type=F32)


def _proj_kernel(x_ref, w_ref, rc_ref, rs1_ref, rs2_ref,
                 q_ref, k_ref, v_ref, hq_ref, ff_ref, fb_ref, hi_ref, hg_ref, ga_ref, gb_ref,
                 *, cuts):
    xb = x_ref[...].astype(BF16)

    def mm(name):
        a, b = cuts[name]
        return _dot(xb, w_ref[:, a:b])

    rc = rc_ref[...]
    rs1 = rs1_ref[...]
    rs2 = rs2_ref[...]

    def rope(blk):
        return blk * rc + pltpu.roll(blk, ROPE_DIM // 2, 1) * rs1 + pltpu.roll(blk, LANES - ROPE_DIM // 2, 1) * rs2

    aq = mm("aq")
    for j in range(aq.shape[1] // LANES):
        blk = rope(aq[:, j * LANES:(j + 1) * LANES]) * (LOG2E * HEAD_DIM ** -0.5)
        q_ref[:, j * LANES:(j + 1) * LANES] = blk.astype(BF16)
    akv = mm("akv")
    ak = rope(akv[:, :LANES])
    k_ref[:, :LANES] = ak.astype(BF16)
    k_ref[:, LANES:] = pltpu.roll(ak, HEAD_DIM, 1).astype(BF16)
    av = akv[:, LANES:]
    v_ref[:, :LANES] = av.astype(BF16)
    v_ref[:, LANES:] = pltpu.roll(av, HEAD_DIM, 1).astype(BF16)

    ga_ref[...] = _sigmoid(mm("ga")).astype(BF16)
    gb_ref[...] = _sigmoid(mm("gb")).astype(BF16)
    hq = mm("hq")
    hq_ref[...] = hq * _sigmoid(hq) * (HG_DIM ** -0.5)
    hg = mm("hg")
    hg_ref[...] = (hg * _sigmoid(hg)).astype(BF16)
    hi_ref[...] = mm("hi").astype(BF16)
    ff_ref[...] = mm("hf_f")
    fb_ref[...] = mm("hf_b")


def _proj(x2d, w_in, rope_tabs, seq_len, d_model):
    n_tok = x2d.shape[0]
    tm = PROJ_ROWS
    att_q = ATT_HEADS * HEAD_DIM
    att_kv = ATT_KV_HEADS * HEAD_DIM
    hg = HG_HEADS * HG_DIM
    sizes = [("aq", att_q), ("akv", 2 * att_kv), ("hq", hg), ("hf_f", hg), ("hf_b", hg),
             ("hi", hg), ("hg", hg), ("ga", d_model), ("gb", d_model)]
    cuts, off = {}, 0
    for name, s in sizes:
        cuts[name] = (off, off + s)
        off += s
    d_in = off
    pos_blocks = seq_len // tm
    row = lambda c: pl.BlockSpec((tm, c), lambda i: (i, 0))
    tab = pl.BlockSpec((tm, LANES), lambda i: (i % pos_blocks, 0))
    out_shapes = [
        jax.ShapeDtypeStruct((n_tok, att_q), BF16),
        jax.ShapeDtypeStruct((n_tok, 2 * att_kv), BF16),
        jax.ShapeDtypeStruct((n_tok, 2 * att_kv), BF16),
        jax.ShapeDtypeStruct((n_tok, hg), F32),
        jax.ShapeDtypeStruct((n_tok, hg), F32),
        jax.ShapeDtypeStruct((n_tok, hg), F32),
        jax.ShapeDtypeStruct((n_tok, hg), BF16),
        jax.ShapeDtypeStruct((n_tok, hg), BF16),
        jax.ShapeDtypeStruct((n_tok, d_model), BF16),
        jax.ShapeDtypeStruct((n_tok, d_model), BF16),
    ]
    return pl.pallas_call(
        functools.partial(_proj_kernel, cuts=cuts),
        grid=(n_tok // tm,),
        in_specs=[row(d_model),
                  pl.BlockSpec((d_model, d_in), lambda i: (0, 0), pipeline_mode=pl.Buffered(1)),
                  tab, tab, tab],
        out_specs=[row(s.shape[1]) for s in out_shapes],
        out_shape=out_shapes,
        compiler_params=pltpu.CompilerParams(dimension_semantics=("parallel",),
                                             vmem_limit_bytes=VMEM_LIMIT_BYTES),
        name="proj",
    )(x2d, w_in, *rope_tabs)


def _attn_bias_table():
    blk = ATT_BLOCK
    kj = np.arange(blk)[:, None]
    qi = np.arange(blk)[None, :]
    prev_ok = kj >= qi
    next_ok = kj <= qi
    none = np.zeros((blk, blk), bool)
    tab = []
    for p_ok, n_ok in ((prev_ok, next_ok), (none, next_ok), (prev_ok, none)):
        m = np.concatenate([p_ok, n_ok], axis=0)
        tab.append(np.where(np.concatenate([m, m], axis=1), 0.0, NEG_BIG))
    return np.stack(tab).astype(np.float32)


def _attn_kernel(sink_ref, q_ref, kp_ref, kc_ref, kn_ref, vp_ref, vc_ref, vn_ref,
                 bias_first_ref, bias_mid_ref, bias_last_ref, o_ref):
    blk = ATT_BLOCK
    n_qb = q_ref.shape[0] // blk
    k = jnp.concatenate([kp_ref[...], kc_ref[...], kn_ref[...]], axis=0)
    rows_kv = k.shape[0]
    lo = lax.broadcasted_iota(jnp.int32, (rows_kv, LANES), 1) < HEAD_DIM
    zero = jnp.zeros((rows_kv, LANES), BF16)
    k_arr, k_rot = k[:, :LANES], k[:, LANES:]
    k_even = [jnp.where(lo, k_arr, zero), jnp.where(lo, k_rot, zero)]
    k_odd = [jnp.where(lo, zero, k_rot), jnp.where(lo, zero, k_arr)]
    v_arr = jnp.concatenate([vp_ref[:, :LANES], vc_ref[:, :LANES], vn_ref[:, :LANES]], axis=0)
    v_t = jnp.transpose(v_arr.astype(F32)).astype(BF16)
    left = lax.broadcasted_iota(jnp.int32, (1, 2 * blk), 1) < blk

    def softmax_parts(s, bias, sink_row):
        s_prev = s[:blk] + bias[:blk]
        s_mid = s[blk:2 * blk]
        s_next = s[2 * blk:] + bias[blk:]
        m = jnp.max(jnp.maximum(jnp.maximum(s_prev, s_mid), s_next), axis=0, keepdims=True)
        m = jnp.maximum(m, sink_row)
        p_prev, p_mid, p_next = jnp.exp2(s_prev - m), jnp.exp2(s_mid - m), jnp.exp2(s_next - m)
        den = jnp.sum(p_prev + p_mid + p_next, axis=0, keepdims=True) + jnp.exp2(sink_row - m)
        return jnp.concatenate([p_prev, p_mid, p_next], axis=0).astype(BF16), 1.0 / den

    n_slab = ATT_HEADS * HEAD_DIM // LANES
    per_kv = n_slab // ATT_KV_HEADS
    bodies = [(t, g) for t in range(n_qb) for g in range(ATT_KV_HEADS)]

    def scores(t, g):
        qr = slice(t * blk, (t + 1) * blk)
        kr = slice(t * blk, (t + 3) * blk)
        slabs = [q_ref[qr, (per_kv * g + u) * LANES:(per_kv * g + u + 1) * LANES] for u in range(per_kv)]
        qs = jnp.concatenate(slabs, axis=0)
        return _dot_tb(k_even[g][kr], qs), _dot_tb(k_odd[g][kr], qs)

    def finish(t, g, s_e, s_o):
        bias = (bias_first_ref if t == 0 else bias_last_ref if t == n_qb - 1 else bias_mid_ref)[...]
        qr = slice(t * blk, (t + 1) * blk)
        kr = slice(t * blk, (t + 3) * blk)
        h0 = 2 * per_kv * g
        sink_e = jnp.where(left, sink_ref[h0], sink_ref[h0 + 2]) * LOG2E
        sink_o = jnp.where(left, sink_ref[h0 + 1], sink_ref[h0 + 3]) * LOG2E
        p_e, r_e = softmax_parts(s_e, bias, sink_e)
        p_o, r_o = softmax_parts(s_o, bias, sink_o)
        vg = v_t[g * HEAD_DIM:(g + 1) * HEAD_DIM, kr]
        o_t = jnp.concatenate([_dot(vg, p_e) * r_e, _dot(vg, p_o) * r_o], axis=0)
        o = jnp.transpose(o_t)
        for u in range(per_kv):
            o_ref[qr, (per_kv * g + u) * LANES:(per_kv * g + u + 1) * LANES] = \
                o[u * blk:(u + 1) * blk].astype(BF16)

    ahead = 2
    pending = [scores(*b) for b in bodies[:ahead]]
    for n, body in enumerate(bodies):
        ready = pending.pop(0)
        if n + ahead < len(bodies):
            pending.append(scores(*bodies[n + ahead]))
        finish(*body, *ready)


def _attn(q, k2, v2, sink, batch, seq_len):
    blk = ATT_BLOCK
    tq = ATT_TILE
    per = tq // blk
    nt = seq_len // tq
    nb = seq_len // blk
    assert nb >= 2
    cq = q.shape[-1]
    ck = k2.shape[-1]
    q3 = q.reshape(batch, seq_len, cq)
    k3 = k2.reshape(batch, seq_len, ck)
    v3 = v2.reshape(batch, seq_len, ck)
    prev = pl.BlockSpec((None, blk, ck), lambda b, i: (b, jnp.maximum(i * per - 1, 0), 0))
    cur = pl.BlockSpec((None, tq, ck), lambda b, i: (b, i, 0))
    nxt = pl.BlockSpec((None, blk, ck), lambda b, i: (b, jnp.minimum((i + 1) * per, nb - 1), 0))
    qspec = pl.BlockSpec((None, tq, cq), lambda b, i: (b, i, 0))
    bias_tab = jnp.asarray(_attn_bias_table())
    bshape = (None,) + bias_tab.shape[1:]
    bias_first = pl.BlockSpec(bshape, lambda b, i: (jnp.where(i == 0, 1, 0), 0, 0))
    bias_mid = pl.BlockSpec(bshape, lambda b, i: (0, 0, 0))
    bias_last = pl.BlockSpec(bshape, lambda b, i: (jnp.where(i == nt - 1, 2, 0), 0, 0))
    out = pl.pallas_call(
        _attn_kernel,
        grid=(batch, nt),
        in_specs=[pl.BlockSpec(memory_space=pltpu.SMEM), qspec, prev, cur, nxt, prev, cur, nxt,
                  bias_first, bias_mid, bias_last],
        out_specs=qspec,
        out_shape=jax.ShapeDtypeStruct((batch, seq_len, cq), BF16),
        compiler_params=pltpu.CompilerParams(dimension_semantics=("parallel", "parallel"),
                                             vmem_limit_bytes=VMEM_LIMIT_BYTES),
        name="attn",
    )(sink, q3, k3, k3, k3, v3, v3, v3, bias_tab, bias_tab, bias_tab)
    return out.reshape(batch * seq_len, cq)


def _hgrn_consts(reverse):
    c, sub = HG_CHUNK, HG_SUB
    n_sub = c // sub
    t = np.arange(c)[:, None]
    s = np.arange(c)[None, :]
    same = (t // sub) == (s // sub)
    upto = (s >= t) if reverse else (s <= t)
    coeff = np.tile((same & upto).astype(np.float32), (1, 2))
    order = list(range(n_sub))[::-1] if reverse else list(range(n_sub))
    cols = [np.broadcast_to((t // sub) == order[a], (c, c)) for a in range(1, n_sub)]
    cols.append(same & upto)
    rmask = np.tile(np.concatenate(cols, axis=1), (HG_SLAB_HEADS, 1)).astype(np.float32)
    return coeff, rmask


def _hgrn_gates(fpre, lb):
    f = lb + (1.0 - lb) / (1.0 + jnp.exp2(fpre * (-LOG2E)))
    g = jnp.log2(f)
    g_hi = g.astype(BF16)
    g_lo = (g - g_hi.astype(F32)).astype(BF16)
    return g_hi, g_lo, 1.0 - f


def _hgrn_diag_exact(q, kk, b_rel, v, ones_bd, reverse):
    c, sub = HG_CHUNK, HG_SUB
    hk = q.shape[1]
    pos = lax.broadcasted_iota(jnp.int32, (c, hk), 0) % sub
    v32 = v.astype(F32)
    acc = jnp.zeros((c, hk), F32)
    for d in range(sub):
        shift = (c - d) % c if reverse else d
        if d == 0:
            k_s, b_s, v_s = kk, b_rel, v32
        else:
            k_s, b_s, v_s = (pltpu.roll(a, shift, 0) for a in (kk, b_rel, v32))
        valid = (pos <= sub - 1 - d) if reverse else (pos >= d)
        e = jnp.exp2(jnp.where(valid, b_rel - b_s, 0.0))
        p = jnp.where(valid, q * k_s * e, 0.0).astype(BF16)
        a = jnp.concatenate([_dot(p[:, j * HG_SLAB:(j + 1) * HG_SLAB], ones_bd)
                             for j in range(hk // HG_SLAB)], axis=1)
        acc = acc + a * v_s
    return acc


def _hgrn_block(q, fpre, v, lb, coeff, rmask, st_ref, reverse):
    c, sub = HG_CHUNK, HG_SUB
    n_sub = c // sub
    n_chunks = q.shape[0] // c
    hk = q.shape[1]
    n_slab = hk // HG_SLAB
    chunks = list(range(n_chunks))[::-1] if reverse else list(range(n_chunks))
    slabs = [slice(j * HG_SLAB, (j + 1) * HG_SLAB) for j in range(n_slab)]
    order = list(range(n_sub))[::-1] if reverse else list(range(n_sub))
    rows = lambda m: slice(m * sub, (m + 1) * sub)
    crow = lambda ci: slice(ci * c, (ci + 1) * c)
    low_half = lax.broadcasted_iota(jnp.int32, (c, LANES), 1) < HG_DIM
    zeros_sub = jnp.zeros((sub, hk), F32)
    zeros_tile = jnp.zeros((c, LANES), BF16)

    def head_tiles(x, j):
        return [(x[:, j * HG_SLAB + (h // 2) * LANES: j * HG_SLAB + (h // 2 + 1) * LANES], h % 2 == 0)
                for h in range(HG_SLAB_HEADS)]

    def keep_half(tile, low):
        return jnp.where(low_half, tile, 0.0) if low else jnp.where(low_half, 0.0, tile)

    def block_diag(tiles):
        blocks = [jnp.concatenate([t, zeros_tile] if h < 2 else [zeros_tile, t], axis=1)
                  for h, t in enumerate(tiles)]
        return jnp.concatenate(blocks, axis=0)

    g_hi, g_lo, kk = _hgrn_gates(fpre, lb)
    flags = []

    def running_sums(n, after=None):
        cf = coeff
        if after is not None:
            bits = pltpu.bitcast(after[0:8, 0:LANES], jnp.uint32)
            zero = pltpu.bitcast(lax.shift_right_logical(lax.shift_right_logical(bits, jnp.uint32(16)), jnp.uint32(16)), F32)
            cf = coeff + jnp.concatenate([zero] * (c // 8), axis=0)
        ci = chunks[n]
        return _dot(cf.astype(BF16), jnp.concatenate([g_hi[crow(ci)], g_lo[crow(ci)]], axis=0))

    def prepare(n, b_rel):
        ci = chunks[n]
        t_row = []
        for a in range(n_sub):
            last = order[a] * sub + (0 if reverse else sub - 1)
            t_row.append(b_rel[last:last + 1])
        e_row = [jnp.exp2(t) for t in t_row]
        q_rel = q[crow(ci)] * jnp.exp2(b_rel)
        unsafe = jnp.min(b_rel, axis=(0, 1), keepdims=True) < -HG_SAFE_LOG2
        flags.append(unsafe)
        k_diag = jnp.where(unsafe, 0.0, kk[crow(ci)] * jnp.exp2(-b_rel))
        k_rest = [None] * n_sub
        for a in range(n_sub):
            m = order[a]
            k_rest[m] = kk[crow(ci)][rows(m)] * jnp.exp2(t_row[a] - b_rel[rows(m)])
        p_out, acc = [None] * n_sub, None
        for a in range(n_sub):
            p_out[a] = acc
            acc = e_row[a] if acc is None else acc * e_row[a]
        gamma = acc
        p_in, acc = [None] * n_sub, None
        for a in reversed(range(n_sub)):
            p_in[a] = acc
            acc = e_row[a] if acc is None else acc * e_row[a]
        q_in, k_end = [None] * n_sub, [None] * n_sub
        for a in range(n_sub):
            m = order[a]
            q_in[m] = q_rel[rows(m)] if p_out[a] is None else q_rel[rows(m)] * p_out[a]
            k_end[m] = k_rest[m] if p_in[a] is None else k_rest[m] * p_in[a]
        variants = []
        for a1 in range(1, n_sub):
            parts, fac = [zeros_sub] * n_sub, None
            for a0 in range(a1 - 1, -1, -1):
                kr = k_rest[order[a0]]
                parts[order[a0]] = kr if fac is None else kr * fac
                fac = e_row[a0] if fac is None else fac * e_row[a0]
            variants.append(jnp.concatenate(parts, axis=0))
        variants.append(k_diag)
        return dict(
            q_rel=q_rel, gamma=gamma,
            q_in=jnp.concatenate(q_in, axis=0).astype(BF16),
            k_end=jnp.concatenate(k_end, axis=0).astype(BF16),
            k_stack=jnp.concatenate(variants, axis=0).astype(BF16))

    def score(p, j):
        lhs = block_diag([keep_half(t, low).astype(BF16) for t, low in head_tiles(p["q_rel"], j)])
        return (_dot_tb(lhs, p["k_stack"][:, slabs[j]]) * rmask).astype(BF16)

    def values(n):
        ci = chunks[n]
        o_all = [_dot(scores[n][j], jnp.concatenate([v[crow(ci), ls]] * n_sub, axis=0))
                 for j, ls in enumerate(slabs)]
        incr = [_dot_ta(v[crow(ci), ls], prep[n]["k_end"][:, ls]) for ls in slabs]
        return o_all, incr

    st = [[st_ref[j * HG_SLAB_HEADS + h] for h in range(HG_SLAB_HEADS)] for j in range(n_slab)]

    def finish(n):
        o_all, incr = vals[n]
        o_slabs = []
        for j, ls in enumerate(slabs):
            o_inter = _dot_tb(prep[n]["q_in"][:, ls], block_diag([t.astype(BF16) for t in st[j]]))
            gam = head_tiles(prep[n]["gamma"], j)
            for h in range(HG_SLAB_HEADS):
                tile = incr[j][h * HG_DIM:(h + 1) * HG_DIM, (h // 2) * LANES:(h // 2 + 1) * LANES]
                st[j][h] = st[j][h] * gam[h][0] + keep_half(tile, h % 2 == 0)
            oa = o_all[j]
            o_intra = jnp.concatenate(
                [jnp.where(low_half, oa[0:c, :LANES], oa[c:2 * c, :LANES]),
                 jnp.where(low_half, oa[2 * c:3 * c, LANES:], oa[3 * c:4 * c, LANES:])], axis=1)
            o_slabs.append(o_intra + o_inter)
        return jnp.concatenate(o_slabs, axis=1)

    prep, scores, vals = [None] * n_chunks, [None] * n_chunks, [None] * n_chunks
    outs = [None] * n_chunks
    n_stage = 4
    for it in range(n_chunks + n_stage - 1):
        if it < n_chunks:
            gate = vals[it - 3][0][0] if it >= 3 else None
            prep[it] = prepare(it, running_sums(it, gate))
        if 0 <= it - 1 < n_chunks:
            scores[it - 1] = [score(prep[it - 1], j) for j in range(n_slab)]
        if 0 <= it - 2 < n_chunks:
            vals[it - 2] = values(it - 2)
        if 0 <= it - 3 < n_chunks:
            outs[chunks[it - 3]] = finish(it - 3)
    for j in range(n_slab):
        for h in range(HG_SLAB_HEADS):
            st_ref[j * HG_SLAB_HEADS + h] = st[j][h]
    any_unsafe = functools.reduce(jnp.maximum, [fl.astype(jnp.int32) for fl in flags])
    return jnp.concatenate(outs, axis=0), any_unsafe


def _hgrn_add_exact_diag(o_ref, q_ref, f_ref, v_ref, lb, coeff, ones_bd, reverse):
    c = HG_CHUNK

    def body(ci, carry):
        r = pl.ds(pl.multiple_of(ci * c, c), c)
        g_hi, g_lo, kk = _hgrn_gates(f_ref[r, :], lb)
        b_rel = _dot(coeff.astype(BF16), jnp.concatenate([g_hi, g_lo], axis=0))
        unsafe = jnp.min(b_rel, axis=(0, 1), keepdims=True) < -HG_SAFE_LOG2
        extra = _hgrn_diag_exact(q_ref[r, :], kk, b_rel, v_ref[r, :], ones_bd, reverse)
        o_ref[r, :] = o_ref[r, :] + jnp.where(unsafe, extra, 0.0)
        return carry

    lax.fori_loop(0, q_ref.shape[0] // c, body, 0)


def _hgrn_fwd_kernel(q_ref, f_ref, v_ref, lb_ref, coeff_ref, rmask_ref, ones_ref, o_ref, st_ref):
    @pl.when(pl.program_id(1) == 0)
    def _():
        st_ref[...] = jnp.zeros_like(st_ref)

    o, any_unsafe = _hgrn_block(q_ref[...], f_ref[...], v_ref[...], lb_ref[...], coeff_ref[...],
                                rmask_ref[...], st_ref, False)
    o_ref[...] = o

    @pl.when(any_unsafe[0, 0] > 0)
    def _():
        _hgrn_add_exact_diag(o_ref, q_ref, f_ref, v_ref, lb_ref[...], coeff_ref[...], ones_ref[...], False)


def _hgrn_bwd_kernel(q_ref, f_ref, v_ref, lb_ref, coeff_ref, rmask_ref, of_ref, gate_ref, ng_ref, ones_ref,
                     o_ref, st_ref, acc_ref):
    @pl.when(pl.program_id(1) == 0)
    def _():
        st_ref[...] = jnp.zeros_like(st_ref)

    o_b, any_unsafe = _hgrn_block(q_ref[...], f_ref[...], v_ref[...], lb_ref[...], coeff_ref[...],
                                  rmask_ref[...], st_ref, True)
    acc_ref[...] = of_ref[...] + o_b
    ones_bd = ones_ref[...]

    @pl.when(any_unsafe[0, 0] > 0)
    def _():
        _hgrn_add_exact_diag(acc_ref, q_ref, f_ref, v_ref, lb_ref[...], coeff_ref[...], ones_bd, True)

    o = acc_ref[...]
    o2 = (o * o).astype(BF16)
    ms = jnp.concatenate([_dot(o2[:, j * HG_SLAB:(j + 1) * HG_SLAB], ones_bd)
                          for j in range(o.shape[1] // HG_SLAB)], axis=1) * (1.0 / HG_DIM)
    on = o * lax.rsqrt(ms + RMS_EPS) * ng_ref[...]
    o_ref[...] = (on * gate_ref[...].astype(F32)).astype(BF16)


def _hgrn(hq, hf_f, hf_b, hi, hg_gate, lower, norm_g, batch, seq_len):
    hk = hq.shape[-1]
    tb = HG_BLOCK
    nblk = seq_len // tb
    r3 = lambda a: a.reshape(batch, seq_len, a.shape[-1])
    const = lambda a: pl.BlockSpec(a.shape, lambda b, j: (0,) * a.ndim)
    fwd_blk = pl.BlockSpec((None, tb, hk), lambda b, j: (b, j, 0))
    bwd_blk = pl.BlockSpec((None, tb, hk), lambda b, j: (b, nblk - 1 - j, 0))
    cparams = pltpu.CompilerParams(dimension_semantics=("parallel", "arbitrary"),
                                   vmem_limit_bytes=VMEM_LIMIT_BYTES)
    state = pltpu.VMEM((hk // HG_DIM, HG_DIM, LANES), F32)
    lb_f, lb_b = lower[0:1], lower[1:2]
    consts_f = [jnp.asarray(a, F32) for a in _hgrn_consts(False)]
    consts_b = [jnp.asarray(a, F32) for a in _hgrn_consts(True)]
    head_of = np.arange(HG_SLAB) // HG_DIM
    ones_bd = jnp.asarray(head_of[:, None] == head_of[None, :], BF16)
    o_f = pl.pallas_call(
        _hgrn_fwd_kernel,
        grid=(batch, nblk),
        in_specs=[fwd_blk, fwd_blk, fwd_blk, const(lb_f)] + [const(a) for a in consts_f] + [const(ones_bd)],
        out_specs=fwd_blk,
        out_shape=jax.ShapeDtypeStruct((batch, seq_len, hk), F32),
        scratch_shapes=[state],
        compiler_params=cparams,
        name="hgrn_fwd",
    )(r3(hq), r3(hf_f), r3(hi), lb_f, *consts_f, ones_bd)
    ng = jnp.tile(norm_g.astype(F32), HG_HEADS)[None, :]
    o = pl.pallas_call(
        _hgrn_bwd_kernel,
        grid=(batch, nblk),
        in_specs=[bwd_blk, bwd_blk, bwd_blk, const(lb_b)] + [const(a) for a in consts_b]
                 + [bwd_blk, bwd_blk, const(ng), const(ones_bd)],
        out_specs=bwd_blk,
        out_shape=jax.ShapeDtypeStruct((batch, seq_len, hk), BF16),
        scratch_shapes=[state, pltpu.VMEM((tb, hk), F32)],
        compiler_params=cparams,
        name="hgrn_bwd",
    )(r3(hq), r3(hf_b), r3(hi), lb_b, *consts_b, o_f, r3(hg_gate), ng, ones_bd)
    return o.reshape(batch * seq_len, hk)


def _layer_norm(y, g, b):
    mu = jnp.mean(y, axis=-1, keepdims=True)
    d = y - mu
    var = jnp.mean(d * d, axis=-1, keepdims=True)
    return d * lax.rsqrt(var + LN_EPS) * g + b


def _mix_ffn_kernel(x_ref, oa_ref, oh_ref, ga_ref, gb_ref, wpa_ref, wph_ref, wout_ref, w1_ref, w2_ref,
                    g1_ref, b1_ref, g2_ref, b2_ref, y_ref):
    tm = x_ref.shape[0]
    halves = [slice(0, tm // 2), slice(tm // 2, tm)]

    def merge(r):
        mixed = (ga_ref[r, :].astype(F32) * _dot(oa_ref[r, :], wpa_ref[...])
                 + gb_ref[r, :].astype(F32) * _dot(oh_ref[r, :], wph_ref[...]))
        return ALPHA * x_ref[r, :] + _dot(mixed.astype(BF16), wout_ref[...])

    def hidden(x1):
        h = jnp.maximum(_dot(x1.astype(BF16), w1_ref[...]), 0.0)
        return (h * h).astype(BF16)

    pre = [merge(r) for r in halves]
    x1, hh = [None, None], [None, None]
    for i in range(2):
        x1[i] = _layer_norm(pre[i], g1_ref[...], b1_ref[...])
        hh[i] = hidden(x1[i])
    z = [ALPHA * x1[i] + _dot(hh[i], w2_ref[...]) for i in range(2)]
    for i, r in enumerate(halves):
        y_ref[r, :] = _layer_norm(z[i], g2_ref[...], b2_ref[...])


def _mix_ffn(x2d, o_att, o_hg, ga, gb, wpa, wph, wout, w1, w2, g1, b1, g2, b2):
    n_tok, d_model = x2d.shape
    tm = FFN_ROWS
    row = lambda c: pl.BlockSpec((tm, c), lambda i: (i, 0))
    resident = lambda a: pl.BlockSpec(a.shape, lambda i: (0, 0), pipeline_mode=pl.Buffered(1))
    vec = lambda a: pl.BlockSpec((1, a.shape[-1]), lambda i: (0, 0))
    v2 = lambda a: a.reshape(1, -1).astype(F32)
    return pl.pallas_call(
        _mix_ffn_kernel,
        grid=(n_tok // tm,),
        in_specs=[row(d_model), row(o_att.shape[1]), row(o_hg.shape[1]), row(d_model), row(d_model),
                  resident(wpa), resident(wph), resident(wout), resident(w1), resident(w2),
                  vec(g1), vec(b1), vec(g2), vec(b2)],
        out_specs=row(d_model),
        out_shape=jax.ShapeDtypeStruct((n_tok, d_model), F32),
        compiler_params=pltpu.CompilerParams(dimension_semantics=("parallel",),
                                             vmem_limit_bytes=VMEM_LIMIT_BYTES),
        name="mix_ffn",
    )(x2d, o_att, o_hg, ga, gb, wpa, wph, wout, w1, w2, v2(g1), v2(b1), v2(g2), v2(b2))


def _rope_tables(seq_len):
    half = ROPE_DIM // 2
    inv = ROPE_THETA ** (-jnp.arange(0, ROPE_DIM, 2, dtype=F32) / ROPE_DIM)
    ang = jnp.arange(seq_len, dtype=F32)[:, None] * inv[None, :]
    cos, sin = jnp.cos(ang), jnp.sin(ang)
    ones = jnp.ones((seq_len, HEAD_DIM - ROPE_DIM), F32)
    zeros_rest = jnp.zeros((seq_len, HEAD_DIM - ROPE_DIM), F32)
    zeros_half = jnp.zeros((seq_len, half), F32)
    c = jnp.concatenate([cos, cos, ones], axis=1)
    s1 = jnp.concatenate([zeros_half, sin, zeros_rest], axis=1)
    s2 = jnp.concatenate([-sin, zeros_half, zeros_rest], axis=1)
    rep = LANES // HEAD_DIM
    return tuple(jnp.tile(t, (1, rep)) for t in (c, s1, s2))


def _trunk(x, params, lower, rope_tabs):
    batch, seq_len, d_model = x.shape
    x2d = x.reshape(batch * seq_len, d_model)
    for l in range(DEPTH):
        p = params[l]
        q, k2, v2, hq, hf_f, hf_b, hi, hg_gate, ga, gb = _proj(x2d, p["w_in"], rope_tabs, seq_len, d_model)
        o_att = _attn(q, k2, v2, p["sink"], batch, seq_len)
        o_hg = _hgrn(hq, hf_f, hf_b, hi, hg_gate, lower[l], p["norm_g"], batch, seq_len)
        x2d = _mix_ffn(x2d, o_att, o_hg, ga, gb, p["wpa"], p["wph"], p["wout"], p["w1"], p["w2"],
                       p["g1"], p["b1"], p["g2"], p["b2"])
    return x2d.reshape(batch, seq_len, d_model)


def kernel(x_prompt, x_sample, w_in, att_sink, hgrn_lb, hgrn_norm_g, w_proj_att, w_proj_hgrn, w_out,
           ln1_g, ln1_b, w_ff1, w_ff2, ln2_g, ln2_b):
    sm = jax.nn.softmax(hgrn_lb.astype(F32), axis=0)
    lower = jnp.cumsum(sm, axis=0) - sm[0:1]
    params = []
    for l in range(DEPTH):
        params.append(dict(
            w_in=w_in[l].astype(BF16), sink=att_sink[l].astype(F32), norm_g=hgrn_norm_g[l],
            wpa=w_proj_att[l].astype(BF16), wph=w_proj_hgrn[l].astype(BF16), wout=w_out[l].astype(BF16),
            w1=w_ff1[l].astype(BF16), w2=w_ff2[l].astype(BF16),
            g1=ln1_g[l], b1=ln1_b[l], g2=ln2_g[l], b2=ln2_b[l]))
    rope_tabs = _rope_tables(x_prompt.shape[1])
    y_prompt = _trunk(x_prompt, params, lower, rope_tabs)
    if x_sample.shape[1] != x_prompt.shape[1]:
        rope_tabs = _rope_tables(x_sample.shape[1])
    y_sample = _trunk(x_sample, params, lower, rope_tabs)
    return (y_prompt, y_sample)
```

```python
import functools

import jax
import jax.numpy as jnp
import numpy as np
from jax import lax
from jax.experimental import pallas as pl
from jax.experimental.pallas import tpu as pltpu

F32 = jnp.float32
BF16 = jnp.bfloat16

ATT_HEADS = 8
ATT_KV_HEADS = 2
HEAD_DIM = 64
ATT_BLOCK = 128
ROPE_THETA = 500000.0
ROPE_DIM = HEAD_DIM // 4
HG_HEADS = 8
HG_DIM = 64
DEPTH = 2
ALPHA = (2 * DEPTH) ** 0.25
LN_EPS = 1e-5
RMS_EPS = 1e-6
NEG_BIG = -1e30
LOG2E = 1.4426950408889634

LANES = 128
MXU_DIM = 256
VMEM_LIMIT_BYTES = 56 * 1024 * 1024

PROJ_ROWS = 512
FFN_ROWS = 512
ATT_TILE = 512
HG_BLOCK = 1024
HG_CHUNK = 64
HG_SUB = 16
HG_SLAB = MXU_DIM
HG_SLAB_HEADS = HG_SLAB // HG_DIM
HG_SAFE_LOG2 = 96.0


def _sigmoid(x):
    return 1.0 / (1.0 + jnp.exp(-x))


def _dot(a, b):
    return jnp.dot(a, b, preferred_element_type=F32)


def _dot_tb(a, b):
    return lax.dot_general(a, b, (((1,), (1,)), ((), ())), preferred_element_type=F32)


def _dot_ta(a, b):
    return lax.dot_general(a, b, (((0,), (0,)), ((), ())), preferred_element_type=F32)


def _proj_kernel(x_ref, w_ref, rc_ref, rs1_ref, rs2_ref,
                 q_ref, k_ref, v_ref, hq_ref, ff_ref, fb_ref, hi_ref, hg_ref, ga_ref, gb_ref,
                 *, cuts):
    tm = x_ref.shape[0]
    for r0 in range(0, tm, MXU_DIM):
        r = slice(r0, r0 + MXU_DIM)
        xb = x_ref[r, :].astype(BF16)

        def mm(name):
            a, b = cuts[name]
            return _dot(xb, w_ref[:, a:b])

        rc = rc_ref[r, :]
        rs1 = rs1_ref[r, :]
        rs2 = rs2_ref[r, :]

        def rope(blk):
            return (blk * rc + pltpu.roll(blk, ROPE_DIM // 2, 1) * rs1
                    + pltpu.roll(blk, LANES - ROPE_DIM // 2, 1) * rs2)

        aq = mm("aq")
        for j in range(aq.shape[1] // LANES):
            blk = rope(aq[:, j * LANES:(j + 1) * LANES]) * (LOG2E * HEAD_DIM ** -0.5)
            q_ref[r, j * LANES:(j + 1) * LANES] = blk.astype(BF16)
        akv = mm("akv")
        ak = rope(akv[:, :LANES])
        k_ref[r, :LANES] = ak.astype(BF16)
        k_ref[r, LANES:] = pltpu.roll(ak, HEAD_DIM, 1).astype(BF16)
        av = akv[:, LANES:]
        v_ref[r, :LANES] = av.astype(BF16)
        v_ref[r, LANES:] = pltpu.roll(av, HEAD_DIM, 1).astype(BF16)

        ga_ref[r, :] = _sigmoid(mm("ga")).astype(BF16)
        gb_ref[r, :] = _sigmoid(mm("gb")).astype(BF16)
        hq = mm("hq")
        hq_ref[r, :] = hq * _sigmoid(hq) * (HG_DIM ** -0.5)
        hg = mm("hg")
        hg_ref[r, :] = (hg * _sigmoid(hg)).astype(BF16)
        hi_ref[r, :] = mm("hi").astype(BF16)
        ff_ref[r, :] = mm("hf_f")
        fb_ref[r, :] = mm("hf_b")


def _proj(x2d, w_in, rope_tabs, seq_len, d_model):
    n_tok = x2d.shape[0]
    tm = PROJ_ROWS
    att_q = ATT_HEADS * HEAD_DIM
    att_kv = ATT_KV_HEADS * HEAD_DIM
    hg = HG_HEADS * HG_DIM
    sizes = [("aq", att_q), ("akv", 2 * att_kv), ("hq", hg), ("hf_f", hg), ("hf_b", hg),
             ("hi", hg), ("hg", hg), ("ga", d_model), ("gb", d_model)]
    cuts, off = {}, 0
    for name, s in sizes:
        cuts[name] = (off, off + s)
        off += s
    d_in = off
    pos_blocks = seq_len // tm
    row = lambda c: pl.BlockSpec((tm, c), lambda i: (i, 0))
    tab = pl.BlockSpec((tm, LANES), lambda i: (i % pos_blocks, 0))
    out_shapes = [
        jax.ShapeDtypeStruct((n_tok, att_q), BF16),
        jax.ShapeDtypeStruct((n_tok, 2 * att_kv), BF16),
        jax.ShapeDtypeStruct((n_tok, 2 * att_kv), BF16),
        jax.ShapeDtypeStruct((n_tok, hg), F32),
        jax.ShapeDtypeStruct((n_tok, hg), F32),
        jax.ShapeDtypeStruct((n_tok, hg), F32),
        jax.ShapeDtypeStruct((n_tok, hg), BF16),
        jax.ShapeDtypeStruct((n_tok, hg), BF16),
        jax.ShapeDtypeStruct((n_tok, d_model), BF16),
        jax.ShapeDtypeStruct((n_tok, d_model), BF16),
    ]
    return pl.pallas_call(
        functools.partial(_proj_kernel, cuts=cuts),
        grid=(n_tok // tm,),
        in_specs=[row(d_model),
                  pl.BlockSpec((d_model, d_in), lambda i: (0, 0), pipeline_mode=pl.Buffered(1)),
                  tab, tab, tab],
        out_specs=[row(s.shape[1]) for s in out_shapes],
        out_shape=out_shapes,
        compiler_params=pltpu.CompilerParams(dimension_semantics=("parallel",),
                                             vmem_limit_bytes=VMEM_LIMIT_BYTES),
        name="proj",
    )(x2d, w_in, *rope_tabs)


def _attn_bias_table():
    blk = ATT_BLOCK
    kj = np.arange(blk)[:, None]
    qi = np.arange(blk)[None, :]
    prev_ok = kj >= qi
    next_ok = kj <= qi
    none = np.zeros((blk, blk), bool)
    tab = []
    for p_ok, n_ok in ((prev_ok, next_ok), (none, next_ok), (prev_ok, none)):
        m = np.concatenate([p_ok, n_ok], axis=0)
        tab.append(np.where(np.concatenate([m, m], axis=1), 0.0, NEG_BIG))
    return np.stack(tab).astype(np.float32)


def _attn_kernel(sink_ref, q_ref, kp_ref, kc_ref, kn_ref, vp_ref, vc_ref, vn_ref,
                 bias_first_ref, bias_mid_ref, bias_last_ref, o_ref):
    blk = ATT_BLOCK
    n_qb = q_ref.shape[0] // blk
    k = jnp.concatenate([kp_ref[...], kc_ref[...], kn_ref[...]], axis=0)
    rows_kv = k.shape[0]
    lo = lax.broadcasted_iota(jnp.int32, (rows_kv, LANES), 1) < HEAD_DIM
    zero = jnp.zeros((rows_kv, LANES), BF16)
    k_arr, k_rot = k[:, :LANES], k[:, LANES:]
    k_even = [jnp.where(lo, k_arr, zero), jnp.where(lo, k_rot, zero)]
    k_odd = [jnp.where(lo, zero, k_rot), jnp.where(lo, zero, k_arr)]
    v_arr = jnp.concatenate([vp_ref[:, :LANES], vc_ref[:, :LANES], vn_ref[:, :LANES]], axis=0)
    v_t = jnp.transpose(v_arr.astype(F32)).astype(BF16)
    left = lax.broadcasted_iota(jnp.int32, (1, 2 * blk), 1) < blk

    def softmax_parts(s, bias, sink_row):
        s_prev = s[:blk] + bias[:blk]
        s_mid = s[blk:2 * blk]
        s_next = s[2 * blk:] + bias[blk:]
        m = jnp.max(jnp.maximum(jnp.maximum(s_prev, s_mid), s_next), axis=0, keepdims=True)
        m = jnp.maximum(m, sink_row)
        p_prev, p_mid, p_next = jnp.exp2(s_prev - m), jnp.exp2(s_mid - m), jnp.exp2(s_next - m)
        den = jnp.sum(p_prev + p_mid + p_next, axis=0, keepdims=True) + jnp.exp2(sink_row - m)
        return jnp.concatenate([p_prev, p_mid, p_next], axis=0).astype(BF16), 1.0 / den

    n_slab = ATT_HEADS * HEAD_DIM // LANES
    per_kv = n_slab // ATT_KV_HEADS
    bodies = [(t, g) for t in range(n_qb) for g in range(ATT_KV_HEADS)]

    def scores(t, g):
        qr = slice(t * blk, (t + 1) * blk)
        kr = slice(t * blk, (t + 3) * blk)
        slabs = [q_ref[qr, (per_kv * g + u) * LANES:(per_kv * g + u + 1) * LANES] for u in range(per_kv)]
        qs = jnp.concatenate(slabs, axis=0)
        return _dot_tb(k_even[g][kr], qs), _dot_tb(k_odd[g][kr], qs)

    def finish(t, g, s_e, s_o):
        bias = (bias_first_ref if t == 0 else bias_last_ref if t == n_qb - 1 else bias_mid_ref)[...]
        qr = slice(t * blk, (t + 1) * blk)
        kr = slice(t * blk, (t + 3) * blk)
        h0 = 2 * per_kv * g
        sink_e = jnp.where(left, sink_ref[h0], sink_ref[h0 + 2]) * LOG2E
        sink_o = jnp.where(left, sink_ref[h0 + 1], sink_ref[h0 + 3]) * LOG2E
        p_e, r_e = softmax_parts(s_e, bias, sink_e)
        p_o, r_o = softmax_parts(s_o, bias, sink_o)
        vg = v_t[g * HEAD_DIM:(g + 1) * HEAD_DIM, kr]
        o_t = jnp.concatenate([_dot(vg, p_e) * r_e, _dot(vg, p_o) * r_o], axis=0)
        o = jnp.transpose(o_t)
        for u in range(per_kv):
            o_ref[qr, (per_kv * g + u) * LANES:(per_kv * g + u + 1) * LANES] = \
                o[u * blk:(u + 1) * blk].astype(BF16)

    ahead = 2
    pending = [scores(*b) for b in bodies[:ahead]]
    for n, body in enumerate(bodies):
        ready = pending.pop(0)
        if n + ahead < len(bodies):
            pending.append(scores(*bodies[n + ahead]))
        finish(*body, *ready)


def _attn(q, k2, v2, sink, batch, seq_len):
    blk = ATT_BLOCK
    tq = ATT_TILE
    per = tq // blk
    nt = seq_len // tq
    nb = seq_len // blk
    assert nb >= 2
    cq = q.shape[-1]
    ck = k2.shape[-1]
    q3 = q.reshape(batch, seq_len, cq)
    k3 = k2.reshape(batch, seq_len, ck)
    v3 = v2.reshape(batch, seq_len, ck)
    prev = pl.BlockSpec((None, blk, ck), lambda b, i: (b, jnp.maximum(i * per - 1, 0), 0))
    cur = pl.BlockSpec((None, tq, ck), lambda b, i: (b, i, 0))
    nxt = pl.BlockSpec((None, blk, ck), lambda b, i: (b, jnp.minimum((i + 1) * per, nb - 1), 0))
    qspec = pl.BlockSpec((None, tq, cq), lambda b, i: (b, i, 0))
    bias_tab = jnp.asarray(_attn_bias_table())
    bshape = (None,) + bias_tab.shape[1:]
    bias_first = pl.BlockSpec(bshape, lambda b, i: (jnp.where(i == 0, 1, 0), 0, 0))
    bias_mid = pl.BlockSpec(bshape, lambda b, i: (0, 0, 0))
    bias_last = pl.BlockSpec(bshape, lambda b, i: (jnp.where(i == nt - 1, 2, 0), 0, 0))
    out = pl.pallas_call(
        _attn_kernel,
        grid=(batch, nt),
        in_specs=[pl.BlockSpec(memory_space=pltpu.SMEM), qspec, prev, cur, nxt, prev, cur, nxt,
                  bias_first, bias_mid, bias_last],
        out_specs=qspec,
        out_shape=jax.ShapeDtypeStruct((batch, seq_len, cq), BF16),
        compiler_params=pltpu.CompilerParams(dimension_semantics=("parallel", "parallel"),
                                             vmem_limit_bytes=VMEM_LIMIT_BYTES),
        name="attn",
    )(sink, q3, k3, k3, k3, v3, v3, v3, bias_tab, bias_tab, bias_tab)
    return out.reshape(batch * seq_len, cq)


def _hgrn_consts(reverse):
    c, sub = HG_CHUNK, HG_SUB
    n_sub = c // sub
    t = np.arange(c)[:, None]
    s = np.arange(c)[None, :]
    same = (t // sub) == (s // sub)
    upto = (s >= t) if reverse else (s <= t)
    coeff = np.tile((same & upto).astype(np.float32), (1, 2))
    order = list(range(n_sub))[::-1] if reverse else list(range(n_sub))
    cols = [np.broadcast_to((t // sub) == order[a], (c, c)) for a in range(1, n_sub)]
    cols.append(same & upto)
    rmask = np.tile(np.concatenate(cols, axis=1), (HG_SLAB_HEADS, 1)).astype(np.float32)
    return coeff, rmask


def _hgrn_gates(fpre, lb):
    f = lb + (1.0 - lb) / (1.0 + jnp.exp2(fpre * (-LOG2E)))
    g = jnp.log2(f)
    g_hi = g.astype(BF16)
    g_lo = (g - g_hi.astype(F32)).astype(BF16)
    return g_hi, g_lo, 1.0 - f


def _hgrn_diag_exact(q, kk, b_rel, v, ones_bd, reverse):
    c, sub = HG_CHUNK, HG_SUB
    hk = q.shape[1]
    pos = lax.broadcasted_iota(jnp.int32, (c, hk), 0) % sub
    v32 = v.astype(F32)
    acc = jnp.zeros((c, hk), F32)
    for d in range(sub):
        shift = (c - d) % c if reverse else d
        if d == 0:
            k_s, b_s, v_s = kk, b_rel, v32
        else:
            k_s, b_s, v_s = (pltpu.roll(a, shift, 0) for a in (kk, b_rel, v32))
        valid = (pos <= sub - 1 - d) if reverse else (pos >= d)
        e = jnp.exp2(jnp.where(valid, b_rel - b_s, 0.0))
        p = jnp.where(valid, q * k_s * e, 0.0).astype(BF16)
        a = jnp.concatenate([_dot(p[:, j * HG_SLAB:(j + 1) * HG_SLAB], ones_bd)
                             for j in range(hk // HG_SLAB)], axis=1)
        acc = acc + a * v_s
    return acc


def _hgrn_block(q, fpre, v, lb, coeff, rmask, st_ref, reverse):
    c, sub = HG_CHUNK, HG_SUB
    n_sub = c // sub
    n_chunks = q.shape[0] // c
    hk = q.shape[1]
    n_slab = hk // HG_SLAB
    chunks = list(range(n_chunks))[::-1] if reverse else list(range(n_chunks))
    slabs = [slice(j * HG_SLAB, (j + 1) * HG_SLAB) for j in range(n_slab)]
    order = list(range(n_sub))[::-1] if reverse else list(range(n_sub))
    rows = lambda m: slice(m * sub, (m + 1) * sub)
    crow = lambda ci: slice(ci * c, (ci + 1) * c)
    low_half = lax.broadcasted_iota(jnp.int32, (c, LANES), 1) < HG_DIM
    zeros_sub = jnp.zeros((sub, hk), F32)
    zeros_tile = jnp.zeros((c, LANES), BF16)

    def head_tiles(x, j):
        return [(x[:, j * HG_SLAB + (h // 2) * LANES: j * HG_SLAB + (h // 2 + 1) * LANES], h % 2 == 0)
                for h in range(HG_SLAB_HEADS)]

    def keep_half(tile, low):
        return jnp.where(low_half, tile, 0.0) if low else jnp.where(low_half, 0.0, tile)

    def block_diag(tiles):
        blocks = [jnp.concatenate([t, zeros_tile] if h < 2 else [zeros_tile, t], axis=1)
                  for h, t in enumerate(tiles)]
        return jnp.concatenate(blocks, axis=0)

    g_hi, g_lo, kk = _hgrn_gates(fpre, lb)
    flags = []

    def running_sums(n, after=None):
        cf = coeff
        if after is not None:
            bits = pltpu.bitcast(after[0:8, 0:LANES], jnp.uint32)
            zero = pltpu.bitcast(lax.shift_right_logical(lax.shift_right_logical(bits, jnp.uint32(16)), jnp.uint32(16)), F32)
            cf = coeff + jnp.concatenate([zero] * (c // 8), axis=0)
        ci = chunks[n]
        return _dot(cf.astype(BF16), jnp.concatenate([g_hi[crow(ci)], g_lo[crow(ci)]], axis=0))

    def prepare(n, b_rel):
        ci = chunks[n]
        t_row = []
        for a in range(n_sub):
            last = order[a] * sub + (0 if reverse else sub - 1)
            t_row.append(b_rel[last:last + 1])
        e_row = [jnp.exp2(t) for t in t_row]
        q_rel = q[crow(ci)] * jnp.exp2(b_rel)
        unsafe = jnp.min(b_rel, axis=(0, 1), keepdims=True) < -HG_SAFE_LOG2
        flags.append(unsafe)
        k_diag = jnp.where(unsafe, 0.0, kk[crow(ci)] * jnp.exp2(-b_rel))
        k_rest = [None] * n_sub
        for a in range(n_sub):
            m = order[a]
            k_rest[m] = kk[crow(ci)][rows(m)] * jnp.exp2(t_row[a] - b_rel[rows(m)])
        p_out, acc = [None] * n_sub, None
        for a in range(n_sub):
            p_out[a] = acc
            acc = e_row[a] if acc is None else acc * e_row[a]
        gamma = acc
        p_in, acc = [None] * n_sub, None
        for a in reversed(range(n_sub)):
            p_in[a] = acc
            acc = e_row[a] if acc is None else acc * e_row[a]
        q_in, k_end = [None] * n_sub, [None] * n_sub
        for a in range(n_sub):
            m = order[a]
            q_in[m] = q_rel[rows(m)] if p_out[a] is None else q_rel[rows(m)] * p_out[a]
            k_end[m] = k_rest[m] if p_in[a] is None else k_rest[m] * p_in[a]
        variants = []
        for a1 in range(1, n_sub):
            parts, fac = [zeros_sub] * n_sub, None
            for a0 in range(a1 - 1, -1, -1):
                kr = k_rest[order[a0]]
                parts[order[a0]] = kr if fac is None else kr * fac
                fac = e_row[a0] if fac is None else fac * e_row[a0]
            variants.append(jnp.concatenate(parts, axis=0))
        variants.append(k_diag)
        return dict(
            q_rel=q_rel, gamma=gamma,
            q_in=jnp.concatenate(q_in, axis=0).astype(BF16),
            k_end=jnp.concatenate(k_end, axis=0).astype(BF16),
            k_stack=jnp.concatenate(variants, axis=0).astype(BF16))

    def score(p, j):
        lhs = block_diag([keep_half(t, low).astype(BF16) for t, low in head_tiles(p["q_rel"], j)])
        return (_dot_tb(lhs, p["k_stack"][:, slabs[j]]) * rmask).astype(BF16)

    def values(n):
        ci = chunks[n]
        o_all = [_dot(scores[n][j], jnp.concatenate([v[crow(ci), ls]] * n_sub, axis=0))
                 for j, ls in enumerate(slabs)]
        incr = [_dot_ta(v[crow(ci), ls], prep[n]["k_end"][:, ls]) for ls in slabs]
        return o_all, incr

    st = [[st_ref[j * HG_SLAB_HEADS + h] for h in range(HG_SLAB_HEADS)] for j in range(n_slab)]

    def finish(n):
        o_all, incr = vals[n]
        o_slabs = []
        for j, ls in enumerate(slabs):
            o_inter = _dot_tb(prep[n]["q_in"][:, ls], block_diag([t.astype(BF16) for t in st[j]]))
            gam = head_tiles(prep[n]["gamma"], j)
            for h in range(HG_SLAB_HEADS):
                tile = incr[j][h * HG_DIM:(h + 1) * HG_DIM, (h // 2) * LANES:(h // 2 + 1) * LANES]
                st[j][h] = st[j][h] * gam[h][0] + keep_half(tile, h % 2 == 0)
            oa = o_all[j]
            o_intra = jnp.concatenate(
                [jnp.where(low_half, oa[0:c, :LANES], oa[c:2 * c, :LANES]),
                 jnp.where(low_half, oa[2 * c:3 * c, LANES:], oa[3 * c:4 * c, LANES:])], axis=1)
            o_slabs.append(o_intra + o_inter)
        return jnp.concatenate(o_slabs, axis=1)

    prep, scores, vals = [None] * n_chunks, [None] * n_chunks, [None] * n_chunks
    outs = [None] * n_chunks
    n_stage = 4
    for it in range(n_chunks + n_stage - 1):
        if it < n_chunks:
            gate = vals[it - 3][0][0] if it >= 3 else None
            prep[it] = prepare(it, running_sums(it, gate))
        if 0 <= it - 1 < n_chunks:
            scores[it - 1] = [score(prep[it - 1], j) for j in range(n_slab)]
        if 0 <= it - 2 < n_chunks:
            vals[it - 2] = values(it - 2)
        if 0 <= it - 3 < n_chunks:
            outs[chunks[it - 3]] = finish(it - 3)
    for j in range(n_slab):
        for h in range(HG_SLAB_HEADS):
            st_ref[j * HG_SLAB_HEADS + h] = st[j][h]
    any_unsafe = functools.reduce(jnp.maximum, [fl.astype(jnp.int32) for fl in flags])
    return jnp.concatenate(outs, axis=0), any_unsafe


def _hgrn_add_exact_diag(o_ref, q_ref, f_ref, v_ref, lb, coeff, ones_bd, reverse):
    c = HG_CHUNK

    def body(ci, carry):
        r = pl.ds(pl.multiple_of(ci * c, c), c)
        g_hi, g_lo, kk = _hgrn_gates(f_ref[r, :], lb)
        b_rel = _dot(coeff.astype(BF16), jnp.concatenate([g_hi, g_lo], axis=0))
        unsafe = jnp.min(b_rel, axis=(0, 1), keepdims=True) < -HG_SAFE_LOG2
        extra = _hgrn_diag_exact(q_ref[r, :], kk, b_rel, v_ref[r, :], ones_bd, reverse)
        o_ref[r, :] = o_ref[r, :] + jnp.where(unsafe, extra, 0.0)
        return carry

    lax.fori_loop(0, q_ref.shape[0] // c, body, 0)


def _hgrn_fwd_kernel(q_ref, f_ref, v_ref, lb_ref, coeff_ref, rmask_ref, ones_ref, o_ref, st_ref):
    @pl.when(pl.program_id(1) == 0)
    def _():
        st_ref[...] = jnp.zeros_like(st_ref)

    o, any_unsafe = _hgrn_block(q_ref[...], f_ref[...], v_ref[...], lb_ref[...], coeff_ref[...],
                                rmask_ref[...], st_ref, False)
    o_ref[...] = o

    @pl.when(any_unsafe[0, 0] > 0)
    def _():
        _hgrn_add_exact_diag(o_ref, q_ref, f_ref, v_ref, lb_ref[...], coeff_ref[...], ones_ref[...], False)


def _hgrn_bwd_kernel(q_ref, f_ref, v_ref, lb_ref, coeff_ref, rmask_ref, of_ref, gate_ref, ng_ref, ones_ref,
                     o_ref, st_ref, acc_ref):
    @pl.when(pl.program_id(1) == 0)
    def _():
        st_ref[...] = jnp.zeros_like(st_ref)

    o_b, any_unsafe = _hgrn_block(q_ref[...], f_ref[...], v_ref[...], lb_ref[...], coeff_ref[...],
                                  rmask_ref[...], st_ref, True)
    acc_ref[...] = of_ref[...] + o_b
    ones_bd = ones_ref[...]

    @pl.when(any_unsafe[0, 0] > 0)
    def _():
        _hgrn_add_exact_diag(acc_ref, q_ref, f_ref, v_ref, lb_ref[...], coeff_ref[...], ones_bd, True)

    o = acc_ref[...]
    o2 = (o * o).astype(BF16)
    ms = jnp.concatenate([_dot(o2[:, j * HG_SLAB:(j + 1) * HG_SLAB], ones_bd)
                          for j in range(o.shape[1] // HG_SLAB)], axis=1) * (1.0 / HG_DIM)
    on = o * lax.rsqrt(ms + RMS_EPS) * ng_ref[...]
    o_ref[...] = (on * gate_ref[...].astype(F32)).astype(BF16)


def _hgrn(hq, hf_f, hf_b, hi, hg_gate, lower, norm_g, batch, seq_len):
    hk = hq.shape[-1]
    tb = HG_BLOCK
    nblk = seq_len // tb
    r3 = lambda a: a.reshape(batch, seq_len, a.shape[-1])
    const = lambda a: pl.BlockSpec(a.shape, lambda b, j: (0,) * a.ndim)
    fwd_blk = pl.BlockSpec((None, tb, hk), lambda b, j: (b, j, 0))
    bwd_blk = pl.BlockSpec((None, tb, hk), lambda b, j: (b, nblk - 1 - j, 0))
    cparams = pltpu.CompilerParams(dimension_semantics=("parallel", "arbitrary"),
                                   vmem_limit_bytes=VMEM_LIMIT_BYTES)
    state = pltpu.VMEM((hk // HG_DIM, HG_DIM, LANES), F32)
    lb_f, lb_b = lower[0:1], lower[1:2]
    consts_f = [jnp.asarray(a, F32) for a in _hgrn_consts(False)]
    consts_b = [jnp.asarray(a, F32) for a in _hgrn_consts(True)]
    head_of = np.arange(HG_SLAB) // HG_DIM
    ones_bd = jnp.asarray(head_of[:, None] == head_of[None, :], BF16)
    o_f = pl.pallas_call(
        _hgrn_fwd_kernel,
        grid=(batch, nblk),
        in_specs=[fwd_blk, fwd_blk, fwd_blk, const(lb_f)] + [const(a) for a in consts_f] + [const(ones_bd)],
        out_specs=fwd_blk,
        out_shape=jax.ShapeDtypeStruct((batch, seq_len, hk), F32),
        scratch_shapes=[state],
        compiler_params=cparams,
        name="hgrn_fwd",
    )(r3(hq), r3(hf_f), r3(hi), lb_f, *consts_f, ones_bd)
    ng = jnp.tile(norm_g.astype(F32), HG_HEADS)[None, :]
    o = pl.pallas_call(
        _hgrn_bwd_kernel,
        grid=(batch, nblk),
        in_specs=[bwd_blk, bwd_blk, bwd_blk, const(lb_b)] + [const(a) for a in consts_b]
                 + [bwd_blk, bwd_blk, const(ng), const(ones_bd)],
        out_specs=bwd_blk,
        out_shape=jax.ShapeDtypeStruct((batch, seq_len, hk), BF16),
        scratch_shapes=[state, pltpu.VMEM((tb, hk), F32)],
        compiler_params=cparams,
        name="hgrn_bwd",
    )(r3(hq), r3(hf_b), r3(hi), lb_b, *consts_b, o_f, r3(hg_gate), ng, ones_bd)
    return o.reshape(batch * seq_len, hk)


def _layer_norm(y, g, b):
    mu = jnp.mean(y, axis=-1, keepdims=True)
    d = y - mu
    var = jnp.mean(d * d, axis=-1, keepdims=True)
    return d * lax.rsqrt(var + LN_EPS) * g + b


def _mix_ffn_kernel(x_ref, oa_ref, oh_ref, ga_ref, gb_ref, wpa_ref, wph_ref, wout_ref, w1_ref, w2_ref,
                    g1_ref, b1_ref, g2_ref, b2_ref, y_ref):
    tm = x_ref.shape[0]
    halves = [slice(0, tm // 2), slice(tm // 2, tm)]

    def merge(r):
        mixed = (ga_ref[r, :].astype(F32) * _dot(oa_ref[r, :], wpa_ref[...])
                 + gb_ref[r, :].astype(F32) * _dot(oh_ref[r, :], wph_ref[...]))
        return ALPHA * x_ref[r, :] + _dot(mixed.astype(BF16), wout_ref[...])

    def hidden(x1):
        h = jnp.maximum(_dot(x1.astype(BF16), w1_ref[...]), 0.0)
        return (h * h).astype(BF16)

    pre = [merge(r) for r in halves]
    x1, hh = [None, None], [None, None]
    for i in range(2):
        x1[i] = _layer_norm(pre[i], g1_ref[...], b1_ref[...])
        hh[i] = hidden(x1[i])
    z = [ALPHA * x1[i] + _dot(hh[i], w2_ref[...]) for i in range(2)]
    for i, r in enumerate(halves):
        y_ref[r, :] = _layer_norm(z[i], g2_ref[...], b2_ref[...])


def _mix_ffn(x2d, o_att, o_hg, ga, gb, wpa, wph, wout, w1, w2, g1, b1, g2, b2):
    n_tok, d_model = x2d.shape
    tm = FFN_ROWS
    row = lambda c: pl.BlockSpec((tm, c), lambda i: (i, 0))
    resident = lambda a: pl.BlockSpec(a.shape, lambda i: (0, 0), pipeline_mode=pl.Buffered(1))
    vec = lambda a: pl.BlockSpec((1, a.shape[-1]), lambda i: (0, 0))
    v2 = lambda a: a.reshape(1, -1).astype(F32)
    return pl.pallas_call(
        _mix_ffn_kernel,
        grid=(n_tok // tm,),
        in_specs=[row(d_model), row(o_att.shape[1]), row(o_hg.shape[1]), row(d_model), row(d_model),
                  resident(wpa), resident(wph), resident(wout), resident(w1), resident(w2),
                  vec(g1), vec(b1), vec(g2), vec(b2)],
        out_specs=row(d_model),
        out_shape=jax.ShapeDtypeStruct((n_tok, d_model), F32),
        compiler_params=pltpu.CompilerParams(dimension_semantics=("parallel",),
                                             vmem_limit_bytes=VMEM_LIMIT_BYTES),
        name="mix_ffn",
    )(x2d, o_att, o_hg, ga, gb, wpa, wph, wout, w1, w2, v2(g1), v2(b1), v2(g2), v2(b2))


def _rope_tables(seq_len):
    half = ROPE_DIM // 2
    inv = ROPE_THETA ** (-jnp.arange(0, ROPE_DIM, 2, dtype=F32) / ROPE_DIM)
    ang = jnp.arange(seq_len, dtype=F32)[:, None] * inv[None, :]
    cos, sin = jnp.cos(ang), jnp.sin(ang)
    ones = jnp.ones((seq_len, HEAD_DIM - ROPE_DIM), F32)
    zeros_rest = jnp.zeros((seq_len, HEAD_DIM - ROPE_DIM), F32)
    zeros_half = jnp.zeros((seq_len, half), F32)
    c = jnp.concatenate([cos, cos, ones], axis=1)
    s1 = jnp.concatenate([zeros_half, sin, zeros_rest], axis=1)
    s2 = jnp.concatenate([-sin, zeros_half, zeros_rest], axis=1)
    rep = LANES // HEAD_DIM
    return tuple(jnp.tile(t, (1, rep)) for t in (c, s1, s2))


def _trunk(x, params, lower, rope_tabs):
    batch, seq_len, d_model = x.shape
    x2d = x.reshape(batch * seq_len, d_model)
    for l in range(DEPTH):
        p = params[l]
        q, k2, v2, hq, hf_f, hf_b, hi, hg_gate, ga, gb = _proj(x2d, p["w_in"], rope_tabs, seq_len, d_model)
        o_att = _attn(q, k2, v2, p["sink"], batch, seq_len)
        o_hg = _hgrn(hq, hf_f, hf_b, hi, hg_gate, lower[l], p["norm_g"], batch, seq_len)
        x2d = _mix_ffn(x2d, o_att, o_hg, ga, gb, p["wpa"], p["wph"], p["wout"], p["w1"], p["w2"],
                       p["g1"], p["b1"], p["g2"], p["b2"])
    return x2d.reshape(batch, seq_len, d_model)


def kernel(x_prompt, x_sample, w_in, att_sink, hgrn_lb, hgrn_norm_g, w_proj_att, w_proj_hgrn, w_out,
           ln1_g, ln1_b, w_ff1, w_ff2, ln2_g, ln2_b):
    sm = jax.nn.softmax(hgrn_lb.astype(F32), axis=0)
    lower = jnp.cumsum(sm, axis=0) - sm[0:1]
    params = []
    for l in range(DEPTH):
        params.append(dict(
            w_in=w_in[l].astype(BF16), sink=att_sink[l].astype(F32), norm_g=hgrn_norm_g[l],
            wpa=w_proj_att[l].astype(BF16), wph=w_proj_hgrn[l].astype(BF16), wout=w_out[l].astype(BF16),
            w1=w_ff1[l].astype(BF16), w2=w_ff2[l].astype(BF16),
            g1=ln1_g[l], b1=ln1_b[l], g2=ln2_g[l], b2=ln2_b[l]))
    rope_tabs = _rope_tables(x_prompt.shape[1])
    y_prompt = _trunk(x_prompt, params, lower, rope_tabs)
    if x_sample.shape[1] != x_prompt.shape[1]:
        rope_tabs = _rope_tables(x_sample.shape[1])
    y_sample = _trunk(x_sample, params, lower, rope_tabs)
    return (y_prompt, y_sample)
```

```python
import functools

import jax
import jax.numpy as jnp
import numpy as np
from jax import lax
from jax.experimental import pallas as pl
from jax.experimental.pallas import tpu as pltpu

F32 = jnp.float32
BF16 = jnp.bfloat16

ATT_HEADS = 8
ATT_KV_HEADS = 2
HEAD_DIM = 64
ATT_BLOCK = 128
ROPE_THETA = 500000.0
ROPE_DIM = HEAD_DIM // 4
HG_HEADS = 8
HG_DIM = 64
DEPTH = 2
ALPHA = (2 * DEPTH) ** 0.25
LN_EPS = 1e-5
RMS_EPS = 1e-6
NEG_BIG = -1e30
LOG2E = 1.4426950408889634

LANES = 128
MXU_DIM = 256
VMEM_LIMIT_BYTES = 56 * 1024 * 1024

PROJ_ROWS = 1024
FFN_ROWS = 512
ATT_TILE = 1024
HG_BLOCK = 1024
HG_CHUNK = 64
HG_SUB = 16
HG_SLAB = MXU_DIM
HG_SLAB_HEADS = HG_SLAB // HG_DIM
HG_SAFE_LOG2 = 96.0


def _sigmoid(x):
    return 1.0 / (1.0 + jnp.exp(-x))


def _dot(a, b):
    return jnp.dot(a, b, preferred_element_type=F32)


def _dot_tb(a, b):
    return lax.dot_general(a, b, (((1,), (1,)), ((), ())), preferred_element_type=F32)


def _dot_ta(a, b):
    return lax.dot_general(a, b, (((0,), (0,)), ((), ())), preferred_element_type=F32)


def _proj_kernel(x_ref, w_ref, rc_ref, rs1_ref, rs2_ref,
                 q_ref, k_ref, v_ref, hq_ref, ff_ref, fb_ref, hi_ref, hg_ref, ga_ref, gb_ref,
                 *, cuts):
    tm = x_ref.shape[0]
    for r0 in range(0, tm, MXU_DIM):
        r = slice(r0, r0 + MXU_DIM)
        xb = x_ref[r, :].astype(BF16)

        def mm(name):
            a, b = cuts[name]
            return _dot(xb, w_ref[:, a:b])

        rc = rc_ref[r, :]
        rs1 = rs1_ref[r, :]
        rs2 = rs2_ref[r, :]

        def rope(blk):
            return (blk * rc + pltpu.roll(blk, ROPE_DIM // 2, 1) * rs1
                    + pltpu.roll(blk, LANES - ROPE_DIM // 2, 1) * rs2)

        aq = mm("aq")
        for j in range(aq.shape[1] // LANES):
            blk = rope(aq[:, j * LANES:(j + 1) * LANES]) * (LOG2E * HEAD_DIM ** -0.5)
            q_ref[r, j * LANES:(j + 1) * LANES] = blk.astype(BF16)
        akv = mm("akv")
        ak = rope(akv[:, :LANES])
        k_ref[r, :LANES] = ak.astype(BF16)
        k_ref[r, LANES:] = pltpu.roll(ak, HEAD_DIM, 1).astype(BF16)
        av = akv[:, LANES:]
        v_ref[r, :LANES] = av.astype(BF16)
        v_ref[r, LANES:] = pltpu.roll(av, HEAD_DIM, 1).astype(BF16)

        ga_ref[r, :] = _sigmoid(mm("ga")).astype(BF16)
        gb_ref[r, :] = _sigmoid(mm("gb")).astype(BF16)
        hq = mm("hq")
        hq_ref[r, :] = hq * _sigmoid(hq) * (HG_DIM ** -0.5)
        hg = mm("hg")
        hg_ref[r, :] = (hg * _sigmoid(hg)).astype(BF16)
        hi_ref[r, :] = mm("hi").astype(BF16)
        ff_ref[r, :] = mm("hf_f")
        fb_ref[r, :] = mm("hf_b")


def _proj(x2d, w_in, rope_tabs, seq_len, d_model):
    n_tok = x2d.shape[0]
    tm = PROJ_ROWS
    att_q = ATT_HEADS * HEAD_DIM
    att_kv = ATT_KV_HEADS * HEAD_DIM
    hg = HG_HEADS * HG_DIM
    sizes = [("aq", att_q), ("akv", 2 * att_kv), ("hq", hg), ("hf_f", hg), ("hf_b", hg),
             ("hi", hg), ("hg", hg), ("ga", d_model), ("gb", d_model)]
    cuts, off = {}, 0
    for name, s in sizes:
        cuts[name] = (off, off + s)
        off += s
    d_in = off
    pos_blocks = seq_len // tm
    row = lambda c: pl.BlockSpec((tm, c), lambda i: (i, 0))
    tab = pl.BlockSpec((tm, LANES), lambda i: (i % pos_blocks, 0))
    out_shapes = [
        jax.ShapeDtypeStruct((n_tok, att_q), BF16),
        jax.ShapeDtypeStruct((n_tok, 2 * att_kv), BF16),
        jax.ShapeDtypeStruct((n_tok, 2 * att_kv), BF16),
        jax.ShapeDtypeStruct((n_tok, hg), F32),
        jax.ShapeDtypeStruct((n_tok, hg), F32),
        jax.ShapeDtypeStruct((n_tok, hg), F32),
        jax.ShapeDtypeStruct((n_tok, hg), BF16),
        jax.ShapeDtypeStruct((n_tok, hg), BF16),
        jax.ShapeDtypeStruct((n_tok, d_model), BF16),
        jax.ShapeDtypeStruct((n_tok, d_model), BF16),
    ]
    return pl.pallas_call(
        functools.partial(_proj_kernel, cuts=cuts),
        grid=(n_tok // tm,),
        in_specs=[row(d_model),
                  pl.BlockSpec((d_model, d_in), lambda i: (0, 0), pipeline_mode=pl.Buffered(1)),
                  tab, tab, tab],
        out_specs=[row(s.shape[1]) for s in out_shapes],
        out_shape=out_shapes,
        compiler_params=pltpu.CompilerParams(dimension_semantics=("parallel",),
                                             vmem_limit_bytes=VMEM_LIMIT_BYTES),
        name="proj",
    )(x2d, w_in, *rope_tabs)


def _attn_bias_table():
    blk = ATT_BLOCK
    kj = np.arange(blk)[:, None]
    qi = np.arange(blk)[None, :]
    prev_ok = kj >= qi
    next_ok = kj <= qi
    none = np.zeros((blk, blk), bool)
    tab = []
    for p_ok, n_ok in ((prev_ok, next_ok), (none, next_ok), (prev_ok, none)):
        m = np.concatenate([p_ok, n_ok], axis=0)
        tab.append(np.where(np.concatenate([m, m], axis=1), 0.0, NEG_BIG))
    return np.stack(tab).astype(np.float32)


def _attn_kernel(sink_ref, q_ref, kp_ref, kc_ref, kn_ref, vp_ref, vc_ref, vn_ref,
                 bias_first_ref, bias_mid_ref, bias_last_ref, o_ref):
    blk = ATT_BLOCK
    n_qb = q_ref.shape[0] // blk
    k = jnp.concatenate([kp_ref[...], kc_ref[...], kn_ref[...]], axis=0)
    rows_kv = k.shape[0]
    lo = lax.broadcasted_iota(jnp.int32, (rows_kv, LANES), 1) < HEAD_DIM
    zero = jnp.zeros((rows_kv, LANES), BF16)
    k_arr, k_rot = k[:, :LANES], k[:, LANES:]
    k_even = [jnp.where(lo, k_arr, zero), jnp.where(lo, k_rot, zero)]
    k_odd = [jnp.where(lo, zero, k_rot), jnp.where(lo, zero, k_arr)]
    v_arr = jnp.concatenate([vp_ref[:, :LANES], vc_ref[:, :LANES], vn_ref[:, :LANES]], axis=0)
    v_t = jnp.transpose(v_arr.astype(F32)).astype(BF16)
    left = lax.broadcasted_iota(jnp.int32, (1, 2 * blk), 1) < blk

    def softmax_parts(s, bias, sink_row):
        s_prev = s[:blk] + bias[:blk]
        s_mid = s[blk:2 * blk]
        s_next = s[2 * blk:] + bias[blk:]
        m = jnp.max(jnp.maximum(jnp.maximum(s_prev, s_mid), s_next), axis=0, keepdims=True)
        m = jnp.maximum(m, sink_row)
        p_prev, p_mid, p_next = jnp.exp2(s_prev - m), jnp.exp2(s_mid - m), jnp.exp2(s_next - m)
        den = jnp.sum(p_prev + p_mid + p_next, axis=0, keepdims=True) + jnp.exp2(sink_row - m)
        return jnp.concatenate([p_prev, p_mid, p_next], axis=0).astype(BF16), 1.0 / den

    n_slab = ATT_HEADS * HEAD_DIM // LANES
    per_kv = n_slab // ATT_KV_HEADS
    bodies = [(t, g) for t in range(n_qb) for g in range(ATT_KV_HEADS)]

    def scores(t, g):
        qr = slice(t * blk, (t + 1) * blk)
        kr = slice(t * blk, (t + 3) * blk)
        slabs = [q_ref[qr, (per_kv * g + u) * LANES:(per_kv * g + u + 1) * LANES] for u in range(per_kv)]
        qs = jnp.concatenate(slabs, axis=0)
        return _dot_tb(k_even[g][kr], qs), _dot_tb(k_odd[g][kr], qs)

    def finish(t, g, s_e, s_o):
        bias = (bias_first_ref if t == 0 else bias_last_ref if t == n_qb - 1 else bias_mid_ref)[...]
        qr = slice(t * blk, (t + 1) * blk)
        kr = slice(t * blk, (t + 3) * blk)
        h0 = 2 * per_kv * g
        sink_e = jnp.where(left, sink_ref[h0], sink_ref[h0 + 2]) * LOG2E
        sink_o = jnp.where(left, sink_ref[h0 + 1], sink_ref[h0 + 3]) * LOG2E
        p_e, r_e = softmax_parts(s_e, bias, sink_e)
        p_o, r_o = softmax_parts(s_o, bias, sink_o)
        vg = v_t[g * HEAD_DIM:(g + 1) * HEAD_DIM, kr]
        o_t = jnp.concatenate([_dot(vg, p_e) * r_e, _dot(vg, p_o) * r_o], axis=0)
        o = jnp.transpose(o_t)
        for u in range(per_kv):
            o_ref[qr, (per_kv * g + u) * LANES:(per_kv * g + u + 1) * LANES] = \
                o[u * blk:(u + 1) * blk].astype(BF16)

    ahead = 2
    pending = [scores(*b) for b in bodies[:ahead]]
    for n, body in enumerate(bodies):
        ready = pending.pop(0)
        if n + ahead < len(bodies):
            pending.append(scores(*bodies[n + ahead]))
        finish(*body, *ready)


def _attn(q, k2, v2, sink, batch, seq_len):
    blk = ATT_BLOCK
    tq = ATT_TILE
    per = tq // blk
    nt = seq_len // tq
    nb = seq_len // blk
    assert nb >= 2
    cq = q.shape[-1]
    ck = k2.shape[-1]
    q3 = q.reshape(batch, seq_len, cq)
    k3 = k2.reshape(batch, seq_len, ck)
    v3 = v2.reshape(batch, seq_len, ck)
    prev = pl.BlockSpec((None, blk, ck), lambda b, i: (b, jnp.maximum(i * per - 1, 0), 0))
    cur = pl.BlockSpec((None, tq, ck), lambda b, i: (b, i, 0))
    nxt = pl.BlockSpec((None, blk, ck), lambda b, i: (b, jnp.minimum((i + 1) * per, nb - 1), 0))
    qspec = pl.BlockSpec((None, tq, cq), lambda b, i: (b, i, 0))
    bias_tab = jnp.asarray(_attn_bias_table())
    bshape = (None,) + bias_tab.shape[1:]
    bias_first = pl.BlockSpec(bshape, lambda b, i: (jnp.where(i == 0, 1, 0), 0, 0))
    bias_mid = pl.BlockSpec(bshape, lambda b, i: (0, 0, 0))
    bias_last = pl.BlockSpec(bshape, lambda b, i: (jnp.where(i == nt - 1, 2, 0), 0, 0))
    out = pl.pallas_call(
        _attn_kernel,
        grid=(batch, nt),
        in_specs=[pl.BlockSpec(memory_space=pltpu.SMEM), qspec, prev, cur, nxt, prev, cur, nxt,
                  bias_first, bias_mid, bias_last],
        out_specs=qspec,
        out_shape=jax.ShapeDtypeStruct((batch, seq_len, cq), BF16),
        compiler_params=pltpu.CompilerParams(dimension_semantics=("parallel", "parallel"),
                                             vmem_limit_bytes=VMEM_LIMIT_BYTES),
        name="attn",
    )(sink, q3, k3, k3, k3, v3, v3, v3, bias_tab, bias_tab, bias_tab)
    return out.reshape(batch * seq_len, cq)


def _hgrn_consts(reverse):
    c, sub = HG_CHUNK, HG_SUB
    n_sub = c // sub
    t = np.arange(c)[:, None]
    s = np.arange(c)[None, :]
    same = (t // sub) == (s // sub)
    upto = (s >= t) if reverse else (s <= t)
    coeff = np.tile((same & upto).astype(np.float32), (1, 2))
    order = list(range(n_sub))[::-1] if reverse else list(range(n_sub))
    cols = [np.broadcast_to((t // sub) == order[a], (c, c)) for a in range(1, n_sub)]
    cols.append(same & upto)
    rmask = np.tile(np.concatenate(cols, axis=1), (HG_SLAB_HEADS, 1)).astype(np.float32)
    return coeff, rmask


def _hgrn_gates(fpre, lb):
    f = lb + (1.0 - lb) / (1.0 + jnp.exp2(fpre * (-LOG2E)))
    g = jnp.log2(f)
    g_hi = g.astype(BF16)
    g_lo = (g - g_hi.astype(F32)).astype(BF16)
    return g_hi, g_lo, 1.0 - f


def _hgrn_diag_exact(q, kk, b_rel, v, ones_bd, reverse):
    c, sub = HG_CHUNK, HG_SUB
    hk = q.shape[1]
    pos = lax.broadcasted_iota(jnp.int32, (c, hk), 0) % sub
    v32 = v.astype(F32)
    acc = jnp.zeros((c, hk), F32)
    for d in range(sub):
        shift = (c - d) % c if reverse else d
        if d == 0:
            k_s, b_s, v_s = kk, b_rel, v32
        else:
            k_s, b_s, v_s = (pltpu.roll(a, shift, 0) for a in (kk, b_rel, v32))
        valid = (pos <= sub - 1 - d) if reverse else (pos >= d)
        e = jnp.exp2(jnp.where(valid, b_rel - b_s, 0.0))
        p = jnp.where(valid, q * k_s * e, 0.0).astype(BF16)
        a = jnp.concatenate([_dot(p[:, j * HG_SLAB:(j + 1) * HG_SLAB], ones_bd)
                             for j in range(hk // HG_SLAB)], axis=1)
        acc = acc + a * v_s
    return acc


def _hgrn_block(q, fpre, v, lb, coeff, rmask, st_ref, reverse):
    c, sub = HG_CHUNK, HG_SUB
    n_sub = c // sub
    n_chunks = q.shape[0] // c
    hk = q.shape[1]
    n_slab = hk // HG_SLAB
    chunks = list(range(n_chunks))[::-1] if reverse else list(range(n_chunks))
    slabs = [slice(j * HG_SLAB, (j + 1) * HG_SLAB) for j in range(n_slab)]
    order = list(range(n_sub))[::-1] if reverse else list(range(n_sub))
    rows = lambda m: slice(m * sub, (m + 1) * sub)
    crow = lambda ci: slice(ci * c, (ci + 1) * c)
    low_half = lax.broadcasted_iota(jnp.int32, (c, LANES), 1) < HG_DIM
    zeros_sub = jnp.zeros((sub, hk), F32)
    zeros_tile = jnp.zeros((c, LANES), BF16)

    def head_tiles(x, j):
        return [(x[:, j * HG_SLAB + (h // 2) * LANES: j * HG_SLAB + (h // 2 + 1) * LANES], h % 2 == 0)
                for h in range(HG_SLAB_HEADS)]

    def keep_half(tile, low):
        return jnp.where(low_half, tile, 0.0) if low else jnp.where(low_half, 0.0, tile)

    def block_diag(tiles):
        blocks = [jnp.concatenate([t, zeros_tile] if h < 2 else [zeros_tile, t], axis=1)
                  for h, t in enumerate(tiles)]
        return jnp.concatenate(blocks, axis=0)

    g_hi, g_lo, kk = _hgrn_gates(fpre, lb)
    flags = []

    def running_sums(n, after=None):
        cf = coeff
        if after is not None:
            bits = pltpu.bitcast(after[0:8, 0:LANES], jnp.uint32)
            zero = pltpu.bitcast(lax.shift_right_logical(lax.shift_right_logical(bits, jnp.uint32(16)), jnp.uint32(16)), F32)
            cf = coeff + jnp.concatenate([zero] * (c // 8), axis=0)
        ci = chunks[n]
        return _dot(cf.astype(BF16), jnp.concatenate([g_hi[crow(ci)], g_lo[crow(ci)]], axis=0))

    def prepare(n, b_rel):
        ci = chunks[n]
        t_row = []
        for a in range(n_sub):
            last = order[a] * sub + (0 if reverse else sub - 1)
            t_row.append(b_rel[last:last + 1])
        e_row = [jnp.exp2(t) for t in t_row]
        q_rel = q[crow(ci)] * jnp.exp2(b_rel)
        unsafe = jnp.min(b_rel, axis=(0, 1), keepdims=True) < -HG_SAFE_LOG2
        flags.append(unsafe)
        k_diag = jnp.where(unsafe, 0.0, kk[crow(ci)] * jnp.exp2(-b_rel))
        k_rest = [None] * n_sub
        for a in range(n_sub):
            m = order[a]
            k_rest[m] = kk[crow(ci)][rows(m)] * jnp.exp2(t_row[a] - b_rel[rows(m)])
        p_out, acc = [None] * n_sub, None
        for a in range(n_sub):
            p_out[a] = acc
            acc = e_row[a] if acc is None else acc * e_row[a]
        gamma = acc
        p_in, acc = [None] * n_sub, None
        for a in reversed(range(n_sub)):
            p_in[a] = acc
            acc = e_row[a] if acc is None else acc * e_row[a]
        q_in, k_end = [None] * n_sub, [None] * n_sub
        for a in range(n_sub):
            m = order[a]
            q_in[m] = q_rel[rows(m)] if p_out[a] is None else q_rel[rows(m)] * p_out[a]
            k_end[m] = k_rest[m] if p_in[a] is None else k_rest[m] * p_in[a]
        variants = []
        for a1 in range(1, n_sub):
            parts, fac = [zeros_sub] * n_sub, None
            for a0 in range(a1 - 1, -1, -1):
                kr = k_rest[order[a0]]
                parts[order[a0]] = kr if fac is None else kr * fac
                fac = e_row[a0] if fac is None else fac * e_row[a0]
            variants.append(jnp.concatenate(parts, axis=0))
        variants.append(k_diag)
        return dict(
            q_rel=q_rel, gamma=gamma,
            q_in=jnp.concatenate(q_in, axis=0).astype(BF16),
            k_end=jnp.concatenate(k_end, axis=0).astype(BF16),
            k_stack=jnp.concatenate(variants, axis=0).astype(BF16))

    def score(p, j):
        lhs = block_diag([keep_half(t, low).astype(BF16) for t, low in head_tiles(p["q_rel"], j)])
        return (_dot_tb(lhs, p["k_stack"][:, slabs[j]]) * rmask).astype(BF16)

    def values(n):
        ci = chunks[n]
        o_all = [_dot(scores[n][j], jnp.concatenate([v[crow(ci), ls]] * n_sub, axis=0))
                 for j, ls in enumerate(slabs)]
        incr = [_dot_ta(v[crow(ci), ls], prep[n]["k_end"][:, ls]) for ls in slabs]
        return o_all, incr

    st = [[st_ref[j * HG_SLAB_HEADS + h] for h in range(HG_SLAB_HEADS)] for j in range(n_slab)]

    def finish(n):
        o_all, incr = vals[n]
        o_slabs = []
        for j, ls in enumerate(slabs):
            o_inter = _dot_tb(prep[n]["q_in"][:, ls], block_diag([t.astype(BF16) for t in st[j]]))
            gam = head_tiles(prep[n]["gamma"], j)
            for h in range(HG_SLAB_HEADS):
                tile = incr[j][h * HG_DIM:(h + 1) * HG_DIM, (h // 2) * LANES:(h // 2 + 1) * LANES]
                st[j][h] = st[j][h] * gam[h][0] + keep_half(tile, h % 2 == 0)
            oa = o_all[j]
            o_intra = jnp.concatenate(
                [jnp.where(low_half, oa[0:c, :LANES], oa[c:2 * c, :LANES]),
                 jnp.where(low_half, oa[2 * c:3 * c, LANES:], oa[3 * c:4 * c, LANES:])], axis=1)
            o_slabs.append(o_intra + o_inter)
        return jnp.concatenate(o_slabs, axis=1)

    prep, scores, vals = [None] * n_chunks, [None] * n_chunks, [None] * n_chunks
    outs = [None] * n_chunks
    n_stage = 4
    for it in range(n_chunks + n_stage - 1):
        if it < n_chunks:
            gate = vals[it - 3][0][0] if it >= 3 else None
            prep[it] = prepare(it, running_sums(it, gate))
        if 0 <= it - 1 < n_chunks:
            scores[it - 1] = [score(prep[it - 1], j) for j in range(n_slab)]
        if 0 <= it - 2 < n_chunks:
            vals[it - 2] = values(it - 2)
        if 0 <= it - 3 < n_chunks:
            outs[chunks[it - 3]] = finish(it - 3)
    for j in range(n_slab):
        for h in range(HG_SLAB_HEADS):
            st_ref[j * HG_SLAB_HEADS + h] = st[j][h]
    any_unsafe = functools.reduce(jnp.maximum, [fl.astype(jnp.int32) for fl in flags])
    return jnp.concatenate(outs, axis=0), any_unsafe


def _hgrn_add_exact_diag(o_ref, q_ref, f_ref, v_ref, lb, coeff, ones_bd, reverse):
    c = HG_CHUNK

    def body(ci, carry):
        r = pl.ds(pl.multiple_of(ci * c, c), c)
        g_hi, g_lo, kk = _hgrn_gates(f_ref[r, :], lb)
        b_rel = _dot(coeff.astype(BF16), jnp.concatenate([g_hi, g_lo], axis=0))
        unsafe = jnp.min(b_rel, axis=(0, 1), keepdims=True) < -HG_SAFE_LOG2
        extra = _hgrn_diag_exact(q_ref[r, :], kk, b_rel, v_ref[r, :], ones_bd, reverse)
        o_ref[r, :] = o_ref[r, :] + jnp.where(unsafe, extra, 0.0)
        return carry

    lax.fori_loop(0, q_ref.shape[0] // c, body, 0)


def _hgrn_fwd_kernel(q_ref, f_ref, v_ref, lb_ref, coeff_ref, rmask_ref, ones_ref, o_ref, st_ref):
    @pl.when(pl.program_id(1) == 0)
    def _():
        st_ref[...] = jnp.zeros_like(st_ref)

    o, any_unsafe = _hgrn_block(q_ref[...], f_ref[...], v_ref[...], lb_ref[...], coeff_ref[...],
                                rmask_ref[...], st_ref, False)
    o_ref[...] = o

    @pl.when(any_unsafe[0, 0] > 0)
    def _():
        _hgrn_add_exact_diag(o_ref, q_ref, f_ref, v_ref, lb_ref[...], coeff_ref[...], ones_ref[...], False)


def _hgrn_bwd_kernel(q_ref, f_ref, v_ref, lb_ref, coeff_ref, rmask_ref, of_ref, gate_ref, ng_ref, ones_ref,
                     o_ref, st_ref, acc_ref):
    @pl.when(pl.program_id(1) == 0)
    def _():
        st_ref[...] = jnp.zeros_like(st_ref)

    o_b, any_unsafe = _hgrn_block(q_ref[...], f_ref[...], v_ref[...], lb_ref[...], coeff_ref[...],
                                  rmask_ref[...], st_ref, True)
    acc_ref[...] = of_ref[...] + o_b
    ones_bd = ones_ref[...]

    @pl.when(any_unsafe[0, 0] > 0)
    def _():
        _hgrn_add_exact_diag(acc_ref, q_ref, f_ref, v_ref, lb_ref[...], coeff_ref[...], ones_bd, True)

    o = acc_ref[...]
    o2 = (o * o).astype(BF16)
    ms = jnp.concatenate([_dot(o2[:, j * HG_SLAB:(j + 1) * HG_SLAB], ones_bd)
                          for j in range(o.shape[1] // HG_SLAB)], axis=1) * (1.0 / HG_DIM)
    on = o * lax.rsqrt(ms + RMS_EPS) * ng_ref[...]
    o_ref[...] = (on * gate_ref[...].astype(F32)).astype(BF16)


def _hgrn(hq, hf_f, hf_b, hi, hg_gate, lower, norm_g, batch, seq_len):
    hk = hq.shape[-1]
    tb = HG_BLOCK
    nblk = seq_len // tb
    r3 = lambda a: a.reshape(batch, seq_len, a.shape[-1])
    const = lambda a: pl.BlockSpec(a.shape, lambda b, j: (0,) * a.ndim)
    fwd_blk = pl.BlockSpec((None, tb, hk), lambda b, j: (b, j, 0))
    bwd_blk = pl.BlockSpec((None, tb, hk), lambda b, j: (b, nblk - 1 - j, 0))
    cparams = pltpu.CompilerParams(dimension_semantics=("parallel", "arbitrary"),
                                   vmem_limit_bytes=VMEM_LIMIT_BYTES)
    state = pltpu.VMEM((hk // HG_DIM, HG_DIM, LANES), F32)
    lb_f, lb_b = lower[0:1], lower[1:2]
    consts_f = [jnp.asarray(a, F32) for a in _hgrn_consts(False)]
    consts_b = [jnp.asarray(a, F32) for a in _hgrn_consts(True)]
    head_of = np.arange(HG_SLAB) // HG_DIM
    ones_bd = jnp.asarray(head_of[:, None] == head_of[None, :], BF16)
    o_f = pl.pallas_call(
        _hgrn_fwd_kernel,
        grid=(batch, nblk),
        in_specs=[fwd_blk, fwd_blk, fwd_blk, const(lb_f)] + [const(a) for a in consts_f] + [const(ones_bd)],
        out_specs=fwd_blk,
        out_shape=jax.ShapeDtypeStruct((batch, seq_len, hk), F32),
        scratch_shapes=[state],
        compiler_params=cparams,
        name="hgrn_fwd",
    )(r3(hq), r3(hf_f), r3(hi), lb_f, *consts_f, ones_bd)
    ng = jnp.tile(norm_g.astype(F32), HG_HEADS)[None, :]
    o = pl.pallas_call(
        _hgrn_bwd_kernel,
        grid=(batch, nblk),
        in_specs=[bwd_blk, bwd_blk, bwd_blk, const(lb_b)] + [const(a) for a in consts_b]
                 + [bwd_blk, bwd_blk, const(ng), const(ones_bd)],
        out_specs=bwd_blk,
        out_shape=jax.ShapeDtypeStruct((batch, seq_len, hk), BF16),
        scratch_shapes=[state, pltpu.VMEM((tb, hk), F32)],
        compiler_params=cparams,
        name="hgrn_bwd",
    )(r3(hq), r3(hf_b), r3(hi), lb_b, *consts_b, o_f, r3(hg_gate), ng, ones_bd)
    return o.reshape(batch * seq_len, hk)


def _layer_norm(y, g, b):
    mu = jnp.mean(y, axis=-1, keepdims=True)
    d = y - mu
    var = jnp.mean(d * d, axis=-1, keepdims=True)
    return d * lax.rsqrt(var + LN_EPS) * g + b


def _mix_ffn_kernel(x_ref, oa_ref, oh_ref, ga_ref, gb_ref, wpa_ref, wph_ref, wout_ref, w1_ref, w2_ref,
                    g1_ref, b1_ref, g2_ref, b2_ref, y_ref):
    tm = x_ref.shape[0]
    halves = [slice(0, tm // 2), slice(tm // 2, tm)]

    def merge(r):
        mixed = (ga_ref[r, :].astype(F32) * _dot(oa_ref[r, :], wpa_ref[...])
                 + gb_ref[r, :].astype(F32) * _dot(oh_ref[r, :], wph_ref[...]))
        return ALPHA * x_ref[r, :] + _dot(mixed.astype(BF16), wout_ref[...])

    def hidden(x1):
        h = jnp.maximum(_dot(x1.astype(BF16), w1_ref[...]), 0.0)
        return (h * h).astype(BF16)

    pre = [merge(r) for r in halves]
    x1, hh = [None, None], [None, None]
    for i in range(2):
        x1[i] = _layer_norm(pre[i], g1_ref[...], b1_ref[...])
        hh[i] = hidden(x1[i])
    z = [ALPHA * x1[i] + _dot(hh[i], w2_ref[...]) for i in range(2)]
    for i, r in enumerate(halves):
        y_ref[r, :] = _layer_norm(z[i], g2_ref[...], b2_ref[...])


def _mix_ffn(x2d, o_att, o_hg, ga, gb, wpa, wph, wout, w1, w2, g1, b1, g2, b2):
    n_tok, d_model = x2d.shape
    tm = FFN_ROWS
    row = lambda c: pl.BlockSpec((tm, c), lambda i: (i, 0))
    resident = lambda a: pl.BlockSpec(a.shape, lambda i: (0, 0), pipeline_mode=pl.Buffered(1))
    vec = lambda a: pl.BlockSpec((1, a.shape[-1]), lambda i: (0, 0))
    v2 = lambda a: a.reshape(1, -1).astype(F32)
    return pl.pallas_call(
        _mix_ffn_kernel,
        grid=(n_tok // tm,),
        in_specs=[row(d_model), row(o_att.shape[1]), row(o_hg.shape[1]), row(d_model), row(d_model),
                  resident(wpa), resident(wph), resident(wout), resident(w1), resident(w2),
                  vec(g1), vec(b1), vec(g2), vec(b2)],
        out_specs=row(d_model),
        out_shape=jax.ShapeDtypeStruct((n_tok, d_model), F32),
        compiler_params=pltpu.CompilerParams(dimension_semantics=("parallel",),
                                             vmem_limit_bytes=VMEM_LIMIT_BYTES),
        name="mix_ffn",
    )(x2d, o_att, o_hg, ga, gb, wpa, wph, wout, w1, w2, v2(g1), v2(b1), v2(g2), v2(b2))


def _rope_tables(seq_len):
    half = ROPE_DIM // 2
    inv = ROPE_THETA ** (-jnp.arange(0, ROPE_DIM, 2, dtype=F32) / ROPE_DIM)
    ang = jnp.arange(seq_len, dtype=F32)[:, None] * inv[None, :]
    cos, sin = jnp.cos(ang), jnp.sin(ang)
    ones = jnp.ones((seq_len, HEAD_DIM - ROPE_DIM), F32)
    zeros_rest = jnp.zeros((seq_len, HEAD_DIM - ROPE_DIM), F32)
    zeros_half = jnp.zeros((seq_len, half), F32)
    c = jnp.concatenate([cos, cos, ones], axis=1)
    s1 = jnp.concatenate([zeros_half, sin, zeros_rest], axis=1)
    s2 = jnp.concatenate([-sin, zeros_half, zeros_rest], axis=1)
    rep = LANES // HEAD_DIM
    return tuple(jnp.tile(t, (1, rep)) for t in (c, s1, s2))


def _trunk(x, params, lower, rope_tabs):
    batch, seq_len, d_model = x.shape
    x2d = x.reshape(batch * seq_len, d_model)
    for l in range(DEPTH):
        p = params[l]
        q, k2, v2, hq, hf_f, hf_b, hi, hg_gate, ga, gb = _proj(x2d, p["w_in"], rope_tabs, seq_len, d_model)
        o_att = _attn(q, k2, v2, p["sink"], batch, seq_len)
        o_hg = _hgrn(hq, hf_f, hf_b, hi, hg_gate, lower[l], p["norm_g"], batch, seq_len)
        x2d = _mix_ffn(x2d, o_att, o_hg, ga, gb, p["wpa"], p["wph"], p["wout"], p["w1"], p["w2"],
                       p["g1"], p["b1"], p["g2"], p["b2"])
    return x2d.reshape(batch, seq_len, d_model)


def kernel(x_prompt, x_sample, w_in, att_sink, hgrn_lb, hgrn_norm_g, w_proj_att, w_proj_hgrn, w_out,
           ln1_g, ln1_b, w_ff1, w_ff2, ln2_g, ln2_b):
    sm = jax.nn.softmax(hgrn_lb.astype(F32), axis=0)
    lower = jnp.cumsum(sm, axis=0) - sm[0:1]
    params = []
    for l in range(DEPTH):
        params.append(dict(
            w_in=w_in[l].astype(BF16), sink=att_sink[l].astype(F32), norm_g=hgrn_norm_g[l],
            wpa=w_proj_att[l].astype(BF16), wph=w_proj_hgrn[l].astype(BF16), wout=w_out[l].astype(BF16),
            w1=w_ff1[l].astype(BF16), w2=w_ff2[l].astype(BF16),
            g1=ln1_g[l], b1=ln1_b[l], g2=ln2_g[l], b2=ln2_b[l]))
    rope_tabs = _rope_tables(x_prompt.shape[1])
    y_prompt = _trunk(x_prompt, params, lower, rope_tabs)
    if x_sample.shape[1] != x_prompt.shape[1]:
        rope_tabs = _rope_tables(x_sample.shape[1])
    y_sample = _trunk(x_sample, params, lower, rope_tabs)
    return (y_prompt, y_sample)
```

```python
import functools

import jax
import jax.numpy as jnp
import numpy as np
from jax import lax
from jax.experimental import pallas as pl
from jax.experimental.pallas import tpu as pltpu

F32 = jnp.float32
BF16 = jnp.bfloat16

ATT_HEADS = 8
ATT_KV_HEADS = 2
HEAD_DIM = 64
ATT_BLOCK = 128
ROPE_THETA = 500000.0
ROPE_DIM = HEAD_DIM // 4
HG_HEADS = 8
HG_DIM = 64
DEPTH = 2
ALPHA = (2 * DEPTH) ** 0.25
LN_EPS = 1e-5
RMS_EPS = 1e-6
NEG_BIG = -1e30
LOG2E = 1.4426950408889634

LANES = 128
MXU_DIM = 256
VMEM_LIMIT_BYTES = 56 * 1024 * 1024

PROJ_ROWS = 1024
FFN_ROWS = 512
ATT_TILE = 1024
HG_BLOCK = 2048
HG_CHUNK = 64
HG_SUB = 16
HG_SLAB = MXU_DIM
HG_SLAB_HEADS = HG_SLAB // HG_DIM
HG_SAFE_LOG2 = 96.0


def _sigmoid(x):
    return 1.0 / (1.0 + jnp.exp(-x))


def _dot(a, b):
    return jnp.dot(a, b, preferred_element_type=F32)


def _dot_tb(a, b):
    return lax.dot_general(a, b, (((1,), (1,)), ((), ())), preferred_element_type=F32)


def _dot_ta(a, b):
    return lax.dot_general(a, b, (((0,), (0,)), ((), ())), preferred_element_type=F32)


def _proj_kernel(x_ref, w_ref, rc_ref, rs1_ref, rs2_ref,
                 q_ref, k_ref, v_ref, hq_ref, ff_ref, fb_ref, hi_ref, hg_ref, ga_ref, gb_ref,
                 *, cuts):
    tm = x_ref.shape[0]
    for r0 in range(0, tm, MXU_DIM):
        r = slice(r0, r0 + MXU_DIM)
        xb = x_ref[r, :].astype(BF16)

        def mm(name):
            a, b = cuts[name]
            return _dot(xb, w_ref[:, a:b])

        rc = rc_ref[r, :]
        rs1 = rs1_ref[r, :]
        rs2 = rs2_ref[r, :]

        def rope(blk):
            return (blk * rc + pltpu.roll(blk, ROPE_DIM // 2, 1) * rs1
                    + pltpu.roll(blk, LANES - ROPE_DIM // 2, 1) * rs2)

        aq = mm("aq")
        for j in range(aq.shape[1] // LANES):
            blk = rope(aq[:, j * LANES:(j + 1) * LANES]) * (LOG2E * HEAD_DIM ** -0.5)
            q_ref[r, j * LANES:(j + 1) * LANES] = blk.astype(BF16)
        akv = mm("akv")
        ak = rope(akv[:, :LANES])
        k_ref[r, :LANES] = ak.astype(BF16)
        k_ref[r, LANES:] = pltpu.roll(ak, HEAD_DIM, 1).astype(BF16)
        av = akv[:, LANES:]
        v_ref[r, :LANES] = av.astype(BF16)
        v_ref[r, LANES:] = pltpu.roll(av, HEAD_DIM, 1).astype(BF16)

        ga_ref[r, :] = _sigmoid(mm("ga")).astype(BF16)
        gb_ref[r, :] = _sigmoid(mm("gb")).astype(BF16)
        hq = mm("hq")
        hq_ref[r, :] = hq * _sigmoid(hq) * (HG_DIM ** -0.5)
        hg = mm("hg")
        hg_ref[r, :] = (hg * _sigmoid(hg)).astype(BF16)
        hi_ref[r, :] = mm("hi").astype(BF16)
        ff_ref[r, :] = mm("hf_f")
        fb_ref[r, :] = mm("hf_b")


def _proj(x2d, w_in, rope_tabs, seq_len, d_model):
    n_tok = x2d.shape[0]
    tm = PROJ_ROWS
    att_q = ATT_HEADS * HEAD_DIM
    att_kv = ATT_KV_HEADS * HEAD_DIM
    hg = HG_HEADS * HG_DIM
    sizes = [("aq", att_q), ("akv", 2 * att_kv), ("hq", hg), ("hf_f", hg), ("hf_b", hg),
             ("hi", hg), ("hg", hg), ("ga", d_model), ("gb", d_model)]
    cuts, off = {}, 0
    for name, s in sizes:
        cuts[name] = (off, off + s)
        off += s
    d_in = off
    pos_blocks = seq_len // tm
    row = lambda c: pl.BlockSpec((tm, c), lambda i: (i, 0))
    tab = pl.BlockSpec((tm, LANES), lambda i: (i % pos_blocks, 0))
    out_shapes = [
        jax.ShapeDtypeStruct((n_tok, att_q), BF16),
        jax.ShapeDtypeStruct((n_tok, 2 * att_kv), BF16),
        jax.ShapeDtypeStruct((n_tok, 2 * att_kv), BF16),
        jax.ShapeDtypeStruct((n_tok, hg), F32),
        jax.ShapeDtypeStruct((n_tok, hg), F32),
        jax.ShapeDtypeStruct((n_tok, hg), F32),
        jax.ShapeDtypeStruct((n_tok, hg), BF16),
        jax.ShapeDtypeStruct((n_tok, hg), BF16),
        jax.ShapeDtypeStruct((n_tok, d_model), BF16),
        jax.ShapeDtypeStruct((n_tok, d_model), BF16),
    ]
    return pl.pallas_call(
        functools.partial(_proj_kernel, cuts=cuts),
        grid=(n_tok // tm,),
        in_specs=[row(d_model),
                  pl.BlockSpec((d_model, d_in), lambda i: (0, 0), pipeline_mode=pl.Buffered(1)),
                  tab, tab, tab],
        out_specs=[row(s.shape[1]) for s in out_shapes],
        out_shape=out_shapes,
        compiler_params=pltpu.CompilerParams(dimension_semantics=("parallel",),
                                             vmem_limit_bytes=VMEM_LIMIT_BYTES),
        name="proj",
    )(x2d, w_in, *rope_tabs)


def _attn_bias_table():
    blk = ATT_BLOCK
    kj = np.arange(blk)[:, None]
    qi = np.arange(blk)[None, :]
    prev_ok = kj >= qi
    next_ok = kj <= qi
    none = np.zeros((blk, blk), bool)
    tab = []
    for p_ok, n_ok in ((prev_ok, next_ok), (none, next_ok), (prev_ok, none)):
        m = np.concatenate([p_ok, n_ok], axis=0)
        tab.append(np.where(np.concatenate([m, m], axis=1), 0.0, NEG_BIG))
    return np.stack(tab).astype(np.float32)


def _attn_kernel(sink_ref, q_ref, kp_ref, kc_ref, kn_ref, vp_ref, vc_ref, vn_ref,
                 bias_first_ref, bias_mid_ref, bias_last_ref, o_ref):
    blk = ATT_BLOCK
    n_qb = q_ref.shape[0] // blk
    k = jnp.concatenate([kp_ref[...], kc_ref[...], kn_ref[...]], axis=0)
    rows_kv = k.shape[0]
    lo = lax.broadcasted_iota(jnp.int32, (rows_kv, LANES), 1) < HEAD_DIM
    zero = jnp.zeros((rows_kv, LANES), BF16)
    k_arr, k_rot = k[:, :LANES], k[:, LANES:]
    k_even = [jnp.where(lo, k_arr, zero), jnp.where(lo, k_rot, zero)]
    k_odd = [jnp.where(lo, zero, k_rot), jnp.where(lo, zero, k_arr)]
    v_arr = jnp.concatenate([vp_ref[:, :LANES], vc_ref[:, :LANES], vn_ref[:, :LANES]], axis=0)
    v_t = jnp.transpose(v_arr.astype(F32)).astype(BF16)
    left = lax.broadcasted_iota(jnp.int32, (1, 2 * blk), 1) < blk

    def softmax_parts(s, bias, sink_row):
        s_prev = s[:blk] + bias[:blk]
        s_mid = s[blk:2 * blk]
        s_next = s[2 * blk:] + bias[blk:]
        m = jnp.max(jnp.maximum(jnp.maximum(s_prev, s_mid), s_next), axis=0, keepdims=True)
        m = jnp.maximum(m, sink_row)
        p_prev, p_mid, p_next = jnp.exp2(s_prev - m), jnp.exp2(s_mid - m), jnp.exp2(s_next - m)
        den = jnp.sum(p_prev + p_mid + p_next, axis=0, keepdims=True) + jnp.exp2(sink_row - m)
        return jnp.concatenate([p_prev, p_mid, p_next], axis=0).astype(BF16), 1.0 / den

    n_slab = ATT_HEADS * HEAD_DIM // LANES
    per_kv = n_slab // ATT_KV_HEADS
    bodies = [(t, g) for t in range(n_qb) for g in range(ATT_KV_HEADS)]

    def scores(t, g):
        qr = slice(t * blk, (t + 1) * blk)
        kr = slice(t * blk, (t + 3) * blk)
        slabs = [q_ref[qr, (per_kv * g + u) * LANES:(per_kv * g + u + 1) * LANES] for u in range(per_kv)]
        qs = jnp.concatenate(slabs, axis=0)
        return _dot_tb(k_even[g][kr], qs), _dot_tb(k_odd[g][kr], qs)

    def finish(t, g, s_e, s_o):
        bias = (bias_first_ref if t == 0 else bias_last_ref if t == n_qb - 1 else bias_mid_ref)[...]
        qr = slice(t * blk, (t + 1) * blk)
        kr = slice(t * blk, (t + 3) * blk)
        h0 = 2 * per_kv * g
        sink_e = jnp.where(left, sink_ref[h0], sink_ref[h0 + 2]) * LOG2E
        sink_o = jnp.where(left, sink_ref[h0 + 1], sink_ref[h0 + 3]) * LOG2E
        p_e, r_e = softmax_parts(s_e, bias, sink_e)
        p_o, r_o = softmax_parts(s_o, bias, sink_o)
        vg = v_t[g * HEAD_DIM:(g + 1) * HEAD_DIM, kr]
        o_t = jnp.concatenate([_dot(vg, p_e) * r_e, _dot(vg, p_o) * r_o], axis=0)
        o = jnp.transpose(o_t)
        for u in range(per_kv):
            o_ref[qr, (per_kv * g + u) * LANES:(per_kv * g + u + 1) * LANES] = \
                o[u * blk:(u + 1) * blk].astype(BF16)

    ahead = 2
    pending = [scores(*b) for b in bodies[:ahead]]
    for n, body in enumerate(bodies):
        ready = pending.pop(0)
        if n + ahead < len(bodies):
            pending.append(scores(*bodies[n + ahead]))
        finish(*body, *ready)


def _attn(q, k2, v2, sink, batch, seq_len):
    blk = ATT_BLOCK
    tq = ATT_TILE
    per = tq // blk
    nt = seq_len // tq
    nb = seq_len // blk
    assert nb >= 2
    cq = q.shape[-1]
    ck = k2.shape[-1]
    q3 = q.reshape(batch, seq_len, cq)
    k3 = k2.reshape(batch, seq_len, ck)
    v3 = v2.reshape(batch, seq_len, ck)
    prev = pl.BlockSpec((None, blk, ck), lambda b, i: (b, jnp.maximum(i * per - 1, 0), 0))
    cur = pl.BlockSpec((None, tq, ck), lambda b, i: (b, i, 0))
    nxt = pl.BlockSpec((None, blk, ck), lambda b, i: (b, jnp.minimum((i + 1) * per, nb - 1), 0))
    qspec = pl.BlockSpec((None, tq, cq), lambda b, i: (b, i, 0))
    bias_tab = jnp.asarray(_attn_bias_table())
    bshape = (None,) + bias_tab.shape[1:]
    bias_first = pl.BlockSpec(bshape, lambda b, i: (jnp.where(i == 0, 1, 0), 0, 0))
    bias_mid = pl.BlockSpec(bshape, lambda b, i: (0, 0, 0))
    bias_last = pl.BlockSpec(bshape, lambda b, i: (jnp.where(i == nt - 1, 2, 0), 0, 0))
    out = pl.pallas_call(
        _attn_kernel,
        grid=(batch, nt),
        in_specs=[pl.BlockSpec(memory_space=pltpu.SMEM), qspec, prev, cur, nxt, prev, cur, nxt,
                  bias_first, bias_mid, bias_last],
        out_specs=qspec,
        out_shape=jax.ShapeDtypeStruct((batch, seq_len, cq), BF16),
        compiler_params=pltpu.CompilerParams(dimension_semantics=("parallel", "parallel"),
                                             vmem_limit_bytes=VMEM_LIMIT_BYTES),
        name="attn",
    )(sink, q3, k3, k3, k3, v3, v3, v3, bias_tab, bias_tab, bias_tab)
    return out.reshape(batch * seq_len, cq)


def _hgrn_consts(reverse):
    c, sub = HG_CHUNK, HG_SUB
    n_sub = c // sub
    t = np.arange(c)[:, None]
    s = np.arange(c)[None, :]
    same = (t // sub) == (s // sub)
    upto = (s >= t) if reverse else (s <= t)
    coeff = np.tile((same & upto).astype(np.float32), (1, 2))
    order = list(range(n_sub))[::-1] if reverse else list(range(n_sub))
    cols = [np.broadcast_to((t // sub) == order[a], (c, c)) for a in range(1, n_sub)]
    cols.append(same & upto)
    rmask = np.tile(np.concatenate(cols, axis=1), (HG_SLAB_HEADS, 1)).astype(np.float32)
    return coeff, rmask


def _hgrn_gates(fpre, lb):
    f = lb + (1.0 - lb) / (1.0 + jnp.exp2(fpre * (-LOG2E)))
    g = jnp.log2(f)
    g_hi = g.astype(BF16)
    g_lo = (g - g_hi.astype(F32)).astype(BF16)
    return g_hi, g_lo, 1.0 - f


def _hgrn_diag_exact(q, kk, b_rel, v, ones_bd, reverse):
    c, sub = HG_CHUNK, HG_SUB
    hk = q.shape[1]
    pos = lax.broadcasted_iota(jnp.int32, (c, hk), 0) % sub
    v32 = v.astype(F32)
    acc = jnp.zeros((c, hk), F32)
    for d in range(sub):
        shift = (c - d) % c if reverse else d
        if d == 0:
            k_s, b_s, v_s = kk, b_rel, v32
        else:
            k_s, b_s, v_s = (pltpu.roll(a, shift, 0) for a in (kk, b_rel, v32))
        valid = (pos <= sub - 1 - d) if reverse else (pos >= d)
        e = jnp.exp2(jnp.where(valid, b_rel - b_s, 0.0))
        p = jnp.where(valid, q * k_s * e, 0.0).astype(BF16)
        a = jnp.concatenate([_dot(p[:, j * HG_SLAB:(j + 1) * HG_SLAB], ones_bd)
                             for j in range(hk // HG_SLAB)], axis=1)
        acc = acc + a * v_s
    return acc


def _hgrn_block(q, fpre, v, lb, coeff, rmask, st_ref, reverse):
    c, sub = HG_CHUNK, HG_SUB
    n_sub = c // sub
    n_chunks = q.shape[0] // c
    hk = q.shape[1]
    n_slab = hk // HG_SLAB
    chunks = list(range(n_chunks))[::-1] if reverse else list(range(n_chunks))
    slabs = [slice(j * HG_SLAB, (j + 1) * HG_SLAB) for j in range(n_slab)]
    order = list(range(n_sub))[::-1] if reverse else list(range(n_sub))
    rows = lambda m: slice(m * sub, (m + 1) * sub)
    crow = lambda ci: slice(ci * c, (ci + 1) * c)
    low_half = lax.broadcasted_iota(jnp.int32, (c, LANES), 1) < HG_DIM
    zeros_sub = jnp.zeros((sub, hk), F32)
    zeros_tile = jnp.zeros((c, LANES), BF16)

    def head_tiles(x, j):
        return [(x[:, j * HG_SLAB + (h // 2) * LANES: j * HG_SLAB + (h // 2 + 1) * LANES], h % 2 == 0)
                for h in range(HG_SLAB_HEADS)]

    def keep_half(tile, low):
        return jnp.where(low_half, tile, 0.0) if low else jnp.where(low_half, 0.0, tile)

    def block_diag(tiles):
        blocks = [jnp.concatenate([t, zeros_tile] if h < 2 else [zeros_tile, t], axis=1)
                  for h, t in enumerate(tiles)]
        return jnp.concatenate(blocks, axis=0)

    g_hi, g_lo, kk = _hgrn_gates(fpre, lb)
    flags = []

    def running_sums(n, after=None):
        cf = coeff
        if after is not None:
            bits = pltpu.bitcast(after[0:8, 0:LANES], jnp.uint32)
            zero = pltpu.bitcast(lax.shift_right_logical(lax.shift_right_logical(bits, jnp.uint32(16)), jnp.uint32(16)), F32)
            cf = coeff + jnp.concatenate([zero] * (c // 8), axis=0)
        ci = chunks[n]
        return _dot(cf.astype(BF16), jnp.concatenate([g_hi[crow(ci)], g_lo[crow(ci)]], axis=0))

    def prepare(n, b_rel):
        ci = chunks[n]
        t_row = []
        for a in range(n_sub):
            last = order[a] * sub + (0 if reverse else sub - 1)
            t_row.append(b_rel[last:last + 1])
        e_row = [jnp.exp2(t) for t in t_row]
        q_rel = q[crow(ci)] * jnp.exp2(b_rel)
        unsafe = jnp.min(b_rel, axis=(0, 1), keepdims=True) < -HG_SAFE_LOG2
        flags.append(unsafe)
        k_diag = jnp.where(unsafe, 0.0, kk[crow(ci)] * jnp.exp2(-b_rel))
        k_rest = [None] * n_sub
        for a in range(n_sub):
            m = order[a]
            k_rest[m] = kk[crow(ci)][rows(m)] * jnp.exp2(t_row[a] - b_rel[rows(m)])
        p_out, acc = [None] * n_sub, None
        for a in range(n_sub):
            p_out[a] = acc
            acc = e_row[a] if acc is None else acc * e_row[a]
        gamma = acc
        p_in, acc = [None] * n_sub, None
        for a in reversed(range(n_sub)):
            p_in[a] = acc
            acc = e_row[a] if acc is None else acc * e_row[a]
        q_in, k_end = [None] * n_sub, [None] * n_sub
        for a in range(n_sub):
            m = order[a]
            q_in[m] = q_rel[rows(m)] if p_out[a] is None else q_rel[rows(m)] * p_out[a]
            k_end[m] = k_rest[m] if p_in[a] is None else k_rest[m] * p_in[a]
        variants = []
        for a1 in range(1, n_sub):
            parts, fac = [zeros_sub] * n_sub, None
            for a0 in range(a1 - 1, -1, -1):
                kr = k_rest[order[a0]]
                parts[order[a0]] = kr if fac is None else kr * fac
                fac = e_row[a0] if fac is None else fac * e_row[a0]
            variants.append(jnp.concatenate(parts, axis=0))
        variants.append(k_diag)
        return dict(
            q_rel=q_rel, gamma=gamma,
            q_in=jnp.concatenate(q_in, axis=0).astype(BF16),
            k_end=jnp.concatenate(k_end, axis=0).astype(BF16),
            k_stack=jnp.concatenate(variants, axis=0).astype(BF16))

    def score(p, j):
        lhs = block_diag([keep_half(t, low).astype(BF16) for t, low in head_tiles(p["q_rel"], j)])
        return (_dot_tb(lhs, p["k_stack"][:, slabs[j]]) * rmask).astype(BF16)

    def values(n):
        ci = chunks[n]
        o_all = [_dot(scores[n][j], jnp.concatenate([v[crow(ci), ls]] * n_sub, axis=0))
                 for j, ls in enumerate(slabs)]
        incr = [_dot_ta(v[crow(ci), ls], prep[n]["k_end"][:, ls]) for ls in slabs]
        return o_all, incr

    st = [[st_ref[j * HG_SLAB_HEADS + h] for h in range(HG_SLAB_HEADS)] for j in range(n_slab)]

    def finish(n):
        o_all, incr = vals[n]
        o_slabs = []
        for j, ls in enumerate(slabs):
            o_inter = _dot_tb(prep[n]["q_in"][:, ls], block_diag([t.astype(BF16) for t in st[j]]))
            gam = head_tiles(prep[n]["gamma"], j)
            for h in range(HG_SLAB_HEADS):
                tile = incr[j][h * HG_DIM:(h + 1) * HG_DIM, (h // 2) * LANES:(h // 2 + 1) * LANES]
                st[j][h] = st[j][h] * gam[h][0] + keep_half(tile, h % 2 == 0)
            oa = o_all[j]
            o_intra = jnp.concatenate(
                [jnp.where(low_half, oa[0:c, :LANES], oa[c:2 * c, :LANES]),
                 jnp.where(low_half, oa[2 * c:3 * c, LANES:], oa[3 * c:4 * c, LANES:])], axis=1)
            o_slabs.append(o_intra + o_inter)
        return jnp.concatenate(o_slabs, axis=1)

    prep, scores, vals = [None] * n_chunks, [None] * n_chunks, [None] * n_chunks
    outs = [None] * n_chunks
    n_stage = 4
    for it in range(n_chunks + n_stage - 1):
        if it < n_chunks:
            gate = vals[it - 3][0][0] if it >= 3 else None
            prep[it] = prepare(it, running_sums(it, gate))
        if 0 <= it - 1 < n_chunks:
            scores[it - 1] = [score(prep[it - 1], j) for j in range(n_slab)]
        if 0 <= it - 2 < n_chunks:
            vals[it - 2] = values(it - 2)
        if 0 <= it - 3 < n_chunks:
            outs[chunks[it - 3]] = finish(it - 3)
    for j in range(n_slab):
        for h in range(HG_SLAB_HEADS):
            st_ref[j * HG_SLAB_HEADS + h] = st[j][h]
    any_unsafe = functools.reduce(jnp.maximum, [fl.astype(jnp.int32) for fl in flags])
    return jnp.concatenate(outs, axis=0), any_unsafe


def _hgrn_add_exact_diag(o_ref, q_ref, f_ref, v_ref, lb, coeff, ones_bd, reverse):
    c = HG_CHUNK

    def body(ci, carry):
        r = pl.ds(pl.multiple_of(ci * c, c), c)
        g_hi, g_lo, kk = _hgrn_gates(f_ref[r, :], lb)
        b_rel = _dot(coeff.astype(BF16), jnp.concatenate([g_hi, g_lo], axis=0))
        unsafe = jnp.min(b_rel, axis=(0, 1), keepdims=True) < -HG_SAFE_LOG2
        extra = _hgrn_diag_exact(q_ref[r, :], kk, b_rel, v_ref[r, :], ones_bd, reverse)
        o_ref[r, :] = o_ref[r, :] + jnp.where(unsafe, extra, 0.0)
        return carry

    lax.fori_loop(0, q_ref.shape[0] // c, body, 0)


def _hgrn_fwd_kernel(q_ref, f_ref, v_ref, lb_ref, coeff_ref, rmask_ref, ones_ref, o_ref, st_ref):
    @pl.when(pl.program_id(1) == 0)
    def _():
        st_ref[...] = jnp.zeros_like(st_ref)

    o, any_unsafe = _hgrn_block(q_ref[...], f_ref[...], v_ref[...], lb_ref[...], coeff_ref[...],
                                rmask_ref[...], st_ref, False)
    o_ref[...] = o

    @pl.when(any_unsafe[0, 0] > 0)
    def _():
        _hgrn_add_exact_diag(o_ref, q_ref, f_ref, v_ref, lb_ref[...], coeff_ref[...], ones_ref[...], False)


def _hgrn_bwd_kernel(q_ref, f_ref, v_ref, lb_ref, coeff_ref, rmask_ref, of_ref, gate_ref, ng_ref, ones_ref,
                     o_ref, st_ref, acc_ref):
    @pl.when(pl.program_id(1) == 0)
    def _():
        st_ref[...] = jnp.zeros_like(st_ref)

    o_b, any_unsafe = _hgrn_block(q_ref[...], f_ref[...], v_ref[...], lb_ref[...], coeff_ref[...],
                                  rmask_ref[...], st_ref, True)
    acc_ref[...] = of_ref[...] + o_b
    ones_bd = ones_ref[...]

    @pl.when(any_unsafe[0, 0] > 0)
    def _():
        _hgrn_add_exact_diag(acc_ref, q_ref, f_ref, v_ref, lb_ref[...], coeff_ref[...], ones_bd, True)

    o = acc_ref[...]
    o2 = (o * o).astype(BF16)
    ms = jnp.concatenate([_dot(o2[:, j * HG_SLAB:(j + 1) * HG_SLAB], ones_bd)
                          for j in range(o.shape[1] // HG_SLAB)], axis=1) * (1.0 / HG_DIM)
    on = o * lax.rsqrt(ms + RMS_EPS) * ng_ref[...]
    o_ref[...] = (on * gate_ref[...].astype(F32)).astype(BF16)


def _hgrn(hq, hf_f, hf_b, hi, hg_gate, lower, norm_g, batch, seq_len):
    hk = hq.shape[-1]
    tb = HG_BLOCK
    nblk = seq_len // tb
    r3 = lambda a: a.reshape(batch, seq_len, a.shape[-1])
    const = lambda a: pl.BlockSpec(a.shape, lambda b, j: (0,) * a.ndim)
    fwd_blk = pl.BlockSpec((None, tb, hk), lambda b, j: (b, j, 0))
    bwd_blk = pl.BlockSpec((None, tb, hk), lambda b, j: (b, nblk - 1 - j, 0))
    cparams = pltpu.CompilerParams(dimension_semantics=("parallel", "arbitrary"),
                                   vmem_limit_bytes=VMEM_LIMIT_BYTES)
    state = pltpu.VMEM((hk // HG_DIM, HG_DIM, LANES), F32)
    lb_f, lb_b = lower[0:1], lower[1:2]
    consts_f = [jnp.asarray(a, F32) for a in _hgrn_consts(False)]
    consts_b = [jnp.asarray(a, F32) for a in _hgrn_consts(True)]
    head_of = np.arange(HG_SLAB) // HG_DIM
    ones_bd = jnp.asarray(head_of[:, None] == head_of[None, :], BF16)
    o_f = pl.pallas_call(
        _hgrn_fwd_kernel,
        grid=(batch, nblk),
        in_specs=[fwd_blk, fwd_blk, fwd_blk, const(lb_f)] + [const(a) for a in consts_f] + [const(ones_bd)],
        out_specs=fwd_blk,
        out_shape=jax.ShapeDtypeStruct((batch, seq_len, hk), F32),
        scratch_shapes=[state],
        compiler_params=cparams,
        name="hgrn_fwd",
    )(r3(hq), r3(hf_f), r3(hi), lb_f, *consts_f, ones_bd)
    ng = jnp.tile(norm_g.astype(F32), HG_HEADS)[None, :]
    o = pl.pallas_call(
        _hgrn_bwd_kernel,
        grid=(batch, nblk),
        in_specs=[bwd_blk, bwd_blk, bwd_blk, const(lb_b)] + [const(a) for a in consts_b]
                 + [bwd_blk, bwd_blk, const(ng), const(ones_bd)],
        out_specs=bwd_blk,
        out_shape=jax.ShapeDtypeStruct((batch, seq_len, hk), BF16),
        scratch_shapes=[state, pltpu.VMEM((tb, hk), F32)],
        compiler_params=cparams,
        name="hgrn_bwd",
    )(r3(hq), r3(hf_b), r3(hi), lb_b, *consts_b, o_f, r3(hg_gate), ng, ones_bd)
    return o.reshape(batch * seq_len, hk)


def _layer_norm(y, g, b):
    mu = jnp.mean(y, axis=-1, keepdims=True)
    d = y - mu
    var = jnp.mean(d * d, axis=-1, keepdims=True)
    return d * lax.rsqrt(var + LN_EPS) * g + b


def _mix_ffn_kernel(x_ref, oa_ref, oh_ref, ga_ref, gb_ref, wpa_ref, wph_ref, wout_ref, w1_ref, w2_ref,
                    g1_ref, b1_ref, g2_ref, b2_ref, y_ref):
    tm = x_ref.shape[0]
    halves = [slice(0, tm // 2), slice(tm // 2, tm)]

    def merge(r):
        mixed = (ga_ref[r, :].astype(F32) * _dot(oa_ref[r, :], wpa_ref[...])
                 + gb_ref[r, :].astype(F32) * _dot(oh_ref[r, :], wph_ref[...]))
        return ALPHA * x_ref[r, :] + _dot(mixed.astype(BF16), wout_ref[...])

    def hidden(x1):
        h = jnp.maximum(_dot(x1.astype(BF16), w1_ref[...]), 0.0)
        return (h * h).astype(BF16)

    pre = [merge(r) for r in halves]
    x1, hh = [None, None], [None, None]
    for i in range(2):
        x1[i] = _layer_norm(pre[i], g1_ref[...], b1_ref[...])
        hh[i] = hidden(x1[i])
    z = [ALPHA * x1[i] + _dot(hh[i], w2_ref[...]) for i in range(2)]
    for i, r in enumerate(halves):
        y_ref[r, :] = _layer_norm(z[i], g2_ref[...], b2_ref[...])


def _mix_ffn(x2d, o_att, o_hg, ga, gb, wpa, wph, wout, w1, w2, g1, b1, g2, b2):
    n_tok, d_model = x2d.shape
    tm = FFN_ROWS
    row = lambda c: pl.BlockSpec((tm, c), lambda i: (i, 0))
    resident = lambda a: pl.BlockSpec(a.shape, lambda i: (0, 0), pipeline_mode=pl.Buffered(1))
    vec = lambda a: pl.BlockSpec((1, a.shape[-1]), lambda i: (0, 0))
    v2 = lambda a: a.reshape(1, -1).astype(F32)
    return pl.pallas_call(
        _mix_ffn_kernel,
        grid=(n_tok // tm,),
        in_specs=[row(d_model), row(o_att.shape[1]), row(o_hg.shape[1]), row(d_model), row(d_model),
                  resident(wpa), resident(wph), resident(wout), resident(w1), resident(w2),
                  vec(g1), vec(b1), vec(g2), vec(b2)],
        out_specs=row(d_model),
        out_shape=jax.ShapeDtypeStruct((n_tok, d_model), F32),
        compiler_params=pltpu.CompilerParams(dimension_semantics=("parallel",),
                                             vmem_limit_bytes=VMEM_LIMIT_BYTES),
        name="mix_ffn",
    )(x2d, o_att, o_hg, ga, gb, wpa, wph, wout, w1, w2, v2(g1), v2(b1), v2(g2), v2(b2))


def _rope_tables(seq_len):
    half = ROPE_DIM // 2
    inv = ROPE_THETA ** (-jnp.arange(0, ROPE_DIM, 2, dtype=F32) / ROPE_DIM)
    ang = jnp.arange(seq_len, dtype=F32)[:, None] * inv[None, :]
    cos, sin = jnp.cos(ang), jnp.sin(ang)
    ones = jnp.ones((seq_len, HEAD_DIM - ROPE_DIM), F32)
    zeros_rest = jnp.zeros((seq_len, HEAD_DIM - ROPE_DIM), F32)
    zeros_half = jnp.zeros((seq_len, half), F32)
    c = jnp.concatenate([cos, cos, ones], axis=1)
    s1 = jnp.concatenate([zeros_half, sin, zeros_rest], axis=1)
    s2 = jnp.concatenate([-sin, zeros_half, zeros_rest], axis=1)
    rep = LANES // HEAD_DIM
    return tuple(jnp.tile(t, (1, rep)) for t in (c, s1, s2))


def _trunk(x, params, lower, rope_tabs):
    batch, seq_len, d_model = x.shape
    x2d = x.reshape(batch * seq_len, d_model)
    for l in range(DEPTH):
        p = params[l]
        q, k2, v2, hq, hf_f, hf_b, hi, hg_gate, ga, gb = _proj(x2d, p["w_in"], rope_tabs, seq_len, d_model)
        o_att = _attn(q, k2, v2, p["sink"], batch, seq_len)
        o_hg = _hgrn(hq, hf_f, hf_b, hi, hg_gate, lower[l], p["norm_g"], batch, seq_len)
        x2d = _mix_ffn(x2d, o_att, o_hg, ga, gb, p["wpa"], p["wph"], p["wout"], p["w1"], p["w2"],
                       p["g1"], p["b1"], p["g2"], p["b2"])
    return x2d.reshape(batch, seq_len, d_model)


def kernel(x_prompt, x_sample, w_in, att_sink, hgrn_lb, hgrn_norm_g, w_proj_att, w_proj_hgrn, w_out,
           ln1_g, ln1_b, w_ff1, w_ff2, ln2_g, ln2_b):
    sm = jax.nn.softmax(hgrn_lb.astype(F32), axis=0)
    lower = jnp.cumsum(sm, axis=0) - sm[0:1]
    params = []
    for l in range(DEPTH):
        params.append(dict(
            w_in=w_in[l].astype(BF16), sink=att_sink[l].astype(F32), norm_g=hgrn_norm_g[l],
            wpa=w_proj_att[l].astype(BF16), wph=w_proj_hgrn[l].astype(BF16), wout=w_out[l].astype(BF16),
            w1=w_ff1[l].astype(BF16), w2=w_ff2[l].astype(BF16),
            g1=ln1_g[l], b1=ln1_b[l], g2=ln2_g[l], b2=ln2_b[l]))
    rope_tabs = _rope_tables(x_prompt.shape[1])
    y_prompt = _trunk(x_prompt, params, lower, rope_tabs)
    if x_sample.shape[1] != x_prompt.shape[1]:
        rope_tabs = _rope_tables(x_sample.shape[1])
    y_sample = _trunk(x_sample, params, lower, rope_tabs)
    return (y_prompt, y_sample)
```

```python
import functools

import jax
import jax.numpy as jnp
import numpy as np
from jax import lax
from jax.experimental import pallas as pl
from jax.experimental.pallas import tpu as pltpu

F32 = jnp.float32
BF16 = jnp.bfloat16

ATT_HEADS = 8
ATT_KV_HEADS = 2
HEAD_DIM = 64
ATT_BLOCK = 128
ROPE_THETA = 500000.0
ROPE_DIM = HEAD_DIM // 4
HG_HEADS = 8
HG_DIM = 64
DEPTH = 2
ALPHA = (2 * DEPTH) ** 0.25
LN_EPS = 1e-5
RMS_EPS = 1e-6
NEG_BIG = -1e30
LOG2E = 1.4426950408889634

LANES = 128
MXU_DIM = 256
VMEM_LIMIT_BYTES = 56 * 1024 * 1024

PROJ_ROWS = 1024
FFN_ROWS = 512
ATT_TILE = 1024
HG_BLOCK = 2048
HG_CHUNK = 64
HG_SUB = 16
HG_SLAB = MXU_DIM
HG_SLAB_HEADS = HG_SLAB // HG_DIM
HG_SAFE_LOG2 = 96.0


def _sigmoid(x):
    return 1.0 / (1.0 + jnp.exp(-x))


def _dot(a, b):
    return jnp.dot(a, b, preferred_element_type=F32)


def _dot_tb(a, b):
    return lax.dot_general(a, b, (((1,), (1,)), ((), ())), preferred_element_type=F32)


def _dot_ta(a, b):
    return lax.dot_general(a, b, (((0,), (0,)), ((), ())), preferred_element_type=F32)


def _proj_kernel(x_ref, w_ref, rc_ref, rs1_ref, rs2_ref,
                 q_ref, k_ref, v_ref, hq_ref, ff_ref, fb_ref, hi_ref, hg_ref, ga_ref, gb_ref,
                 *, cuts):
    tm = x_ref.shape[0]
    for r0 in range(0, tm, MXU_DIM):
        r = slice(r0, r0 + MXU_DIM)
        xb = x_ref[r, :].astype(BF16)

        def mm(name):
            a, b = cuts[name]
            return _dot(xb, w_ref[:, a:b])

        rc = rc_ref[r, :]
        rs1 = rs1_ref[r, :]
        rs2 = rs2_ref[r, :]

        def rope(blk):
            return (blk * rc + pltpu.roll(blk, ROPE_DIM // 2, 1) * rs1
                    + pltpu.roll(blk, LANES - ROPE_DIM // 2, 1) * rs2)

        aq = mm("aq")
        for j in range(aq.shape[1] // LANES):
            blk = rope(aq[:, j * LANES:(j + 1) * LANES]) * (LOG2E * HEAD_DIM ** -0.5)
            q_ref[r, j * LANES:(j + 1) * LANES] = blk.astype(BF16)
        akv = mm("akv")
        ak = rope(akv[:, :LANES])
        k_ref[r, :LANES] = ak.astype(BF16)
        k_ref[r, LANES:] = pltpu.roll(ak, HEAD_DIM, 1).astype(BF16)
        av = akv[:, LANES:]
        v_ref[r, :LANES] = av.astype(BF16)
        v_ref[r, LANES:] = pltpu.roll(av, HEAD_DIM, 1).astype(BF16)

        ga_ref[r, :] = _sigmoid(mm("ga")).astype(BF16)
        gb_ref[r, :] = _sigmoid(mm("gb")).astype(BF16)
        hq = mm("hq")
        hq_ref[r, :] = (hq * _sigmoid(hq) * (HG_DIM ** -0.5)).astype(BF16)
        hg = mm("hg")
        hg_ref[r, :] = (hg * _sigmoid(hg)).astype(BF16)
        hi_ref[r, :] = mm("hi").astype(BF16)
        ff_ref[r, :] = mm("hf_f")
        fb_ref[r, :] = mm("hf_b")


def _proj(x2d, w_in, rope_tabs, seq_len, d_model):
    n_tok = x2d.shape[0]
    tm = PROJ_ROWS
    att_q = ATT_HEADS * HEAD_DIM
    att_kv = ATT_KV_HEADS * HEAD_DIM
    hg = HG_HEADS * HG_DIM
    sizes = [("aq", att_q), ("akv", 2 * att_kv), ("hq", hg), ("hf_f", hg), ("hf_b", hg),
             ("hi", hg), ("hg", hg), ("ga", d_model), ("gb", d_model)]
    cuts, off = {}, 0
    for name, s in sizes:
        cuts[name] = (off, off + s)
        off += s
    d_in = off
    pos_blocks = seq_len // tm
    row = lambda c: pl.BlockSpec((tm, c), lambda i: (i, 0))
    tab = pl.BlockSpec((tm, LANES), lambda i: (i % pos_blocks, 0))
    out_shapes = [
        jax.ShapeDtypeStruct((n_tok, att_q), BF16),
        jax.ShapeDtypeStruct((n_tok, 2 * att_kv), BF16),
        jax.ShapeDtypeStruct((n_tok, 2 * att_kv), BF16),
        jax.ShapeDtypeStruct((n_tok, hg), BF16),
        jax.ShapeDtypeStruct((n_tok, hg), F32),
        jax.ShapeDtypeStruct((n_tok, hg), F32),
        jax.ShapeDtypeStruct((n_tok, hg), BF16),
        jax.ShapeDtypeStruct((n_tok, hg), BF16),
        jax.ShapeDtypeStruct((n_tok, d_model), BF16),
        jax.ShapeDtypeStruct((n_tok, d_model), BF16),
    ]
    return pl.pallas_call(
        functools.partial(_proj_kernel, cuts=cuts),
        grid=(n_tok // tm,),
        in_specs=[row(d_model),
                  pl.BlockSpec((d_model, d_in), lambda i: (0, 0), pipeline_mode=pl.Buffered(1)),
                  tab, tab, tab],
        out_specs=[row(s.shape[1]) for s in out_shapes],
        out_shape=out_shapes,
        compiler_params=pltpu.CompilerParams(dimension_semantics=("parallel",),
                                             vmem_limit_bytes=VMEM_LIMIT_BYTES),
        name="proj",
    )(x2d, w_in, *rope_tabs)


def _attn_bias_table():
    blk = ATT_BLOCK
    kj = np.arange(blk)[:, None]
    qi = np.arange(blk)[None, :]
    prev_ok = kj >= qi
    next_ok = kj <= qi
    none = np.zeros((blk, blk), bool)
    tab = []
    for p_ok, n_ok in ((prev_ok, next_ok), (none, next_ok), (prev_ok, none)):
        m = np.concatenate([p_ok, n_ok], axis=0)
        tab.append(np.where(np.concatenate([m, m], axis=1), 0.0, NEG_BIG))
    return np.stack(tab).astype(np.float32)


def _attn_kernel(sink_ref, q_ref, kp_ref, kc_ref, kn_ref, vp_ref, vc_ref, vn_ref,
                 bias_first_ref, bias_mid_ref, bias_last_ref, o_ref):
    blk = ATT_BLOCK
    n_qb = q_ref.shape[0] // blk
    k = jnp.concatenate([kp_ref[...], kc_ref[...], kn_ref[...]], axis=0)
    rows_kv = k.shape[0]
    lo = lax.broadcasted_iota(jnp.int32, (rows_kv, LANES), 1) < HEAD_DIM
    zero = jnp.zeros((rows_kv, LANES), BF16)
    k_arr, k_rot = k[:, :LANES], k[:, LANES:]
    k_even = [jnp.where(lo, k_arr, zero), jnp.where(lo, k_rot, zero)]
    k_odd = [jnp.where(lo, zero, k_rot), jnp.where(lo, zero, k_arr)]
    v_arr = jnp.concatenate([vp_ref[:, :LANES], vc_ref[:, :LANES], vn_ref[:, :LANES]], axis=0)
    v_t = jnp.transpose(v_arr.astype(F32)).astype(BF16)
    left = lax.broadcasted_iota(jnp.int32, (1, 2 * blk), 1) < blk

    def softmax_parts(s, bias, sink_row):
        s_prev = s[:blk] + bias[:blk]
        s_mid = s[blk:2 * blk]
        s_next = s[2 * blk:] + bias[blk:]
        m = jnp.max(jnp.maximum(jnp.maximum(s_prev, s_mid), s_next), axis=0, keepdims=True)
        m = jnp.maximum(m, sink_row)
        p_prev, p_mid, p_next = jnp.exp2(s_prev - m), jnp.exp2(s_mid - m), jnp.exp2(s_next - m)
        den = jnp.sum(p_prev + p_mid + p_next, axis=0, keepdims=True) + jnp.exp2(sink_row - m)
        return jnp.concatenate([p_prev, p_mid, p_next], axis=0).astype(BF16), 1.0 / den

    n_slab = ATT_HEADS * HEAD_DIM // LANES
    per_kv = n_slab // ATT_KV_HEADS
    bodies = [(t, g) for t in range(n_qb) for g in range(ATT_KV_HEADS)]

    def scores(t, g):
        qr = slice(t * blk, (t + 1) * blk)
        kr = slice(t * blk, (t + 3) * blk)
        slabs = [q_ref[qr, (per_kv * g + u) * LANES:(per_kv * g + u + 1) * LANES] for u in range(per_kv)]
        qs = jnp.concatenate(slabs, axis=0)
        return _dot_tb(k_even[g][kr], qs), _dot_tb(k_odd[g][kr], qs)

    def finish(t, g, s_e, s_o):
        bias = (bias_first_ref if t == 0 else bias_last_ref if t == n_qb - 1 else bias_mid_ref)[...]
        qr = slice(t * blk, (t + 1) * blk)
        kr = slice(t * blk, (t + 3) * blk)
        h0 = 2 * per_kv * g
        sink_e = jnp.where(left, sink_ref[h0], sink_ref[h0 + 2]) * LOG2E
        sink_o = jnp.where(left, sink_ref[h0 + 1], sink_ref[h0 + 3]) * LOG2E
        p_e, r_e = softmax_parts(s_e, bias, sink_e)
        p_o, r_o = softmax_parts(s_o, bias, sink_o)
        vg = v_t[g * HEAD_DIM:(g + 1) * HEAD_DIM, kr]
        o_t = jnp.concatenate([_dot(vg, p_e) * r_e, _dot(vg, p_o) * r_o], axis=0)
        o = jnp.transpose(o_t)
        for u in range(per_kv):
            o_ref[qr, (per_kv * g + u) * LANES:(per_kv * g + u + 1) * LANES] = \
                o[u * blk:(u + 1) * blk].astype(BF16)

    ahead = 2
    pending = [scores(*b) for b in bodies[:ahead]]
    for n, body in enumerate(bodies):
        ready = pending.pop(0)
        if n + ahead < len(bodies):
            pending.append(scores(*bodies[n + ahead]))
        finish(*body, *ready)


def _attn(q, k2, v2, sink, batch, seq_len):
    blk = ATT_BLOCK
    tq = ATT_TILE
    per = tq // blk
    nt = seq_len // tq
    nb = seq_len // blk
    assert nb >= 2
    cq = q.shape[-1]
    ck = k2.shape[-1]
    q3 = q.reshape(batch, seq_len, cq)
    k3 = k2.reshape(batch, seq_len, ck)
    v3 = v2.reshape(batch, seq_len, ck)
    prev = pl.BlockSpec((None, blk, ck), lambda b, i: (b, jnp.maximum(i * per - 1, 0), 0))
    cur = pl.BlockSpec((None, tq, ck), lambda b, i: (b, i, 0))
    nxt = pl.BlockSpec((None, blk, ck), lambda b, i: (b, jnp.minimum((i + 1) * per, nb - 1), 0))
    qspec = pl.BlockSpec((None, tq, cq), lambda b, i: (b, i, 0))
    bias_tab = jnp.asarray(_attn_bias_table())
    bshape = (None,) + bias_tab.shape[1:]
    bias_first = pl.BlockSpec(bshape, lambda b, i: (jnp.where(i == 0, 1, 0), 0, 0))
    bias_mid = pl.BlockSpec(bshape, lambda b, i: (0, 0, 0))
    bias_last = pl.BlockSpec(bshape, lambda b, i: (jnp.where(i == nt - 1, 2, 0), 0, 0))
    out = pl.pallas_call(
        _attn_kernel,
        grid=(batch, nt),
        in_specs=[pl.BlockSpec(memory_space=pltpu.SMEM), qspec, prev, cur, nxt, prev, cur, nxt,
                  bias_first, bias_mid, bias_last],
        out_specs=qspec,
        out_shape=jax.ShapeDtypeStruct((batch, seq_len, cq), BF16),
        compiler_params=pltpu.CompilerParams(dimension_semantics=("parallel", "parallel"),
                                             vmem_limit_bytes=VMEM_LIMIT_BYTES),
        name="attn",
    )(sink, q3, k3, k3, k3, v3, v3, v3, bias_tab, bias_tab, bias_tab)
    return out.reshape(batch * seq_len, cq)


def _hgrn_consts(reverse):
    c, sub = HG_CHUNK, HG_SUB
    n_sub = c // sub
    t = np.arange(c)[:, None]
    s = np.arange(c)[None, :]
    same = (t // sub) == (s // sub)
    upto = (s >= t) if reverse else (s <= t)
    coeff = np.tile((same & upto).astype(np.float32), (1, 2))
    order = list(range(n_sub))[::-1] if reverse else list(range(n_sub))
    cols = [np.broadcast_to((t // sub) == order[a], (c, c)) for a in range(1, n_sub)]
    cols.append(same & upto)
    rmask = np.tile(np.concatenate(cols, axis=1), (HG_SLAB_HEADS, 1)).astype(np.float32)
    return coeff, rmask


def _hgrn_gates(fpre, lb):
    f = lb + (1.0 - lb) / (1.0 + jnp.exp2(fpre * (-LOG2E)))
    g = jnp.log2(f)
    g_hi = g.astype(BF16)
    g_lo = (g - g_hi.astype(F32)).astype(BF16)
    return g_hi, g_lo, 1.0 - f


def _hgrn_diag_exact(q, kk, b_rel, v, ones_bd, reverse):
    c, sub = HG_CHUNK, HG_SUB
    hk = q.shape[1]
    pos = lax.broadcasted_iota(jnp.int32, (c, hk), 0) % sub
    v32 = v.astype(F32)
    acc = jnp.zeros((c, hk), F32)
    for d in range(sub):
        shift = (c - d) % c if reverse else d
        if d == 0:
            k_s, b_s, v_s = kk, b_rel, v32
        else:
            k_s, b_s, v_s = (pltpu.roll(a, shift, 0) for a in (kk, b_rel, v32))
        valid = (pos <= sub - 1 - d) if reverse else (pos >= d)
        e = jnp.exp2(jnp.where(valid, b_rel - b_s, 0.0))
        p = jnp.where(valid, q * k_s * e, 0.0).astype(BF16)
        a = jnp.concatenate([_dot(p[:, j * HG_SLAB:(j + 1) * HG_SLAB], ones_bd)
                             for j in range(hk // HG_SLAB)], axis=1)
        acc = acc + a * v_s
    return acc


def _hgrn_block(q, fpre, v, lb, coeff, rmask, st_ref, reverse):
    c, sub = HG_CHUNK, HG_SUB
    n_sub = c // sub
    n_chunks = q.shape[0] // c
    hk = q.shape[1]
    n_slab = hk // HG_SLAB
    chunks = list(range(n_chunks))[::-1] if reverse else list(range(n_chunks))
    slabs = [slice(j * HG_SLAB, (j + 1) * HG_SLAB) for j in range(n_slab)]
    order = list(range(n_sub))[::-1] if reverse else list(range(n_sub))
    rows = lambda m: slice(m * sub, (m + 1) * sub)
    crow = lambda ci: slice(ci * c, (ci + 1) * c)
    low_half = lax.broadcasted_iota(jnp.int32, (c, LANES), 1) < HG_DIM
    zeros_sub = jnp.zeros((sub, hk), F32)
    zeros_tile = jnp.zeros((c, LANES), BF16)

    def head_tiles(x, j):
        return [(x[:, j * HG_SLAB + (h // 2) * LANES: j * HG_SLAB + (h // 2 + 1) * LANES], h % 2 == 0)
                for h in range(HG_SLAB_HEADS)]

    def keep_half(tile, low):
        return jnp.where(low_half, tile, 0.0) if low else jnp.where(low_half, 0.0, tile)

    def block_diag(tiles):
        blocks = [jnp.concatenate([t, zeros_tile] if h < 2 else [zeros_tile, t], axis=1)
                  for h, t in enumerate(tiles)]
        return jnp.concatenate(blocks, axis=0)

    g_hi, g_lo, kk = _hgrn_gates(fpre, lb)
    flags = []

    def running_sums(n, after=None):
        cf = coeff
        if after is not None:
            bits = pltpu.bitcast(after[0:8, 0:LANES], jnp.uint32)
            zero = pltpu.bitcast(lax.shift_right_logical(lax.shift_right_logical(bits, jnp.uint32(16)), jnp.uint32(16)), F32)
            cf = coeff + jnp.concatenate([zero] * (c // 8), axis=0)
        ci = chunks[n]
        return _dot(cf.astype(BF16), jnp.concatenate([g_hi[crow(ci)], g_lo[crow(ci)]], axis=0))

    def prepare(n, b_rel):
        ci = chunks[n]
        t_row = []
        for a in range(n_sub):
            last = order[a] * sub + (0 if reverse else sub - 1)
            t_row.append(b_rel[last:last + 1])
        e_row = [jnp.exp2(t) for t in t_row]
        q_rel = q[crow(ci)].astype(F32) * jnp.exp2(b_rel)
        unsafe = jnp.min(b_rel, axis=(0, 1), keepdims=True) < -HG_SAFE_LOG2
        flags.append(unsafe)
        k_diag = jnp.where(unsafe, 0.0, kk[crow(ci)] * jnp.exp2(-b_rel))
        k_rest = [None] * n_sub
        for a in range(n_sub):
            m = order[a]
            k_rest[m] = kk[crow(ci)][rows(m)] * jnp.exp2(t_row[a] - b_rel[rows(m)])
        p_out, acc = [None] * n_sub, None
        for a in range(n_sub):
            p_out[a] = acc
            acc = e_row[a] if acc is None else acc * e_row[a]
        gamma = acc
        p_in, acc = [None] * n_sub, None
        for a in reversed(range(n_sub)):
            p_in[a] = acc
            acc = e_row[a] if acc is None else acc * e_row[a]
        q_in, k_end = [None] * n_sub, [None] * n_sub
        for a in range(n_sub):
            m = order[a]
            q_in[m] = q_rel[rows(m)] if p_out[a] is None else q_rel[rows(m)] * p_out[a]
            k_end[m] = k_rest[m] if p_in[a] is None else k_rest[m] * p_in[a]
        variants = []
        for a1 in range(1, n_sub):
            parts, fac = [zeros_sub] * n_sub, None
            for a0 in range(a1 - 1, -1, -1):
                kr = k_rest[order[a0]]
                parts[order[a0]] = kr if fac is None else kr * fac
                fac = e_row[a0] if fac is None else fac * e_row[a0]
            variants.append(jnp.concatenate(parts, axis=0))
        variants.append(k_diag)
        return dict(
            q_rel=q_rel, gamma=gamma,
            q_in=jnp.concatenate(q_in, axis=0).astype(BF16),
            k_end=jnp.concatenate(k_end, axis=0).astype(BF16),
            k_stack=jnp.concatenate(variants, axis=0).astype(BF16))

    def score(p, j):
        lhs = block_diag([keep_half(t, low).astype(BF16) for t, low in head_tiles(p["q_rel"], j)])
        return (_dot_tb(lhs, p["k_stack"][:, slabs[j]]) * rmask).astype(BF16)

    def values(n):
        ci = chunks[n]
        o_all = [_dot(scores[n][j], jnp.concatenate([v[crow(ci), ls]] * n_sub, axis=0))
                 for j, ls in enumerate(slabs)]
        incr = [_dot_ta(v[crow(ci), ls], prep[n]["k_end"][:, ls]) for ls in slabs]
        return o_all, incr

    st = [[st_ref[j * HG_SLAB_HEADS + h] for h in range(HG_SLAB_HEADS)] for j in range(n_slab)]

    def finish(n):
        o_all, incr = vals[n]
        o_slabs = []
        for j, ls in enumerate(slabs):
            o_inter = _dot_tb(prep[n]["q_in"][:, ls], block_diag([t.astype(BF16) for t in st[j]]))
            gam = head_tiles(prep[n]["gamma"], j)
            for h in range(HG_SLAB_HEADS):
                tile = incr[j][h * HG_DIM:(h + 1) * HG_DIM, (h // 2) * LANES:(h // 2 + 1) * LANES]
                st[j][h] = st[j][h] * gam[h][0] + keep_half(tile, h % 2 == 0)
            oa = o_all[j]
            o_intra = jnp.concatenate(
                [jnp.where(low_half, oa[0:c, :LANES], oa[c:2 * c, :LANES]),
                 jnp.where(low_half, oa[2 * c:3 * c, LANES:], oa[3 * c:4 * c, LANES:])], axis=1)
            o_slabs.append(o_intra + o_inter)
        return jnp.concatenate(o_slabs, axis=1)

    prep, scores, vals = [None] * n_chunks, [None] * n_chunks, [None] * n_chunks
    outs = [None] * n_chunks
    n_stage = 4
    for it in range(n_chunks + n_stage - 1):
        if it < n_chunks:
            gate = vals[it - 3][0][0] if it >= 3 else None
            prep[it] = prepare(it, running_sums(it, gate))
        if 0 <= it - 1 < n_chunks:
            scores[it - 1] = [score(prep[it - 1], j) for j in range(n_slab)]
        if 0 <= it - 2 < n_chunks:
            vals[it - 2] = values(it - 2)
        if 0 <= it - 3 < n_chunks:
            outs[chunks[it - 3]] = finish(it - 3)
    for j in range(n_slab):
        for h in range(HG_SLAB_HEADS):
            st_ref[j * HG_SLAB_HEADS + h] = st[j][h]
    any_unsafe = functools.reduce(jnp.maximum, [fl.astype(jnp.int32) for fl in flags])
    return jnp.concatenate(outs, axis=0), any_unsafe


def _hgrn_add_exact_diag(o_ref, q_ref, f_ref, v_ref, lb, coeff, ones_bd, reverse):
    c = HG_CHUNK

    def body(ci, carry):
        r = pl.ds(pl.multiple_of(ci * c, c), c)
        g_hi, g_lo, kk = _hgrn_gates(f_ref[r, :], lb)
        b_rel = _dot(coeff.astype(BF16), jnp.concatenate([g_hi, g_lo], axis=0))
        unsafe = jnp.min(b_rel, axis=(0, 1), keepdims=True) < -HG_SAFE_LOG2
        extra = _hgrn_diag_exact(q_ref[r, :].astype(F32), kk, b_rel, v_ref[r, :], ones_bd, reverse)
        o_ref[r, :] = o_ref[r, :] + jnp.where(unsafe, extra, 0.0)
        return carry

    lax.fori_loop(0, q_ref.shape[0] // c, body, 0)


def _hgrn_fwd_kernel(q_ref, f_ref, v_ref, lb_ref, coeff_ref, rmask_ref, ones_ref, o_ref, st_ref):
    @pl.when(pl.program_id(1) == 0)
    def _():
        st_ref[...] = jnp.zeros_like(st_ref)

    o, any_unsafe = _hgrn_block(q_ref[...], f_ref[...], v_ref[...], lb_ref[...], coeff_ref[...],
                                rmask_ref[...], st_ref, False)
    o_ref[...] = o

    @pl.when(any_unsafe[0, 0] > 0)
    def _():
        _hgrn_add_exact_diag(o_ref, q_ref, f_ref, v_ref, lb_ref[...], coeff_ref[...], ones_ref[...], False)


def _hgrn_bwd_kernel(q_ref, f_ref, v_ref, lb_ref, coeff_ref, rmask_ref, of_ref, gate_ref, ng_ref, ones_ref,
                     o_ref, st_ref, acc_ref):
    @pl.when(pl.program_id(1) == 0)
    def _():
        st_ref[...] = jnp.zeros_like(st_ref)

    o_b, any_unsafe = _hgrn_block(q_ref[...], f_ref[...], v_ref[...], lb_ref[...], coeff_ref[...],
                                  rmask_ref[...], st_ref, True)
    acc_ref[...] = of_ref[...] + o_b
    ones_bd = ones_ref[...]

    @pl.when(any_unsafe[0, 0] > 0)
    def _():
        _hgrn_add_exact_diag(acc_ref, q_ref, f_ref, v_ref, lb_ref[...], coeff_ref[...], ones_bd, True)

    o = acc_ref[...]
    o2 = (o * o).astype(BF16)
    ms = jnp.concatenate([_dot(o2[:, j * HG_SLAB:(j + 1) * HG_SLAB], ones_bd)
                          for j in range(o.shape[1] // HG_SLAB)], axis=1) * (1.0 / HG_DIM)
    on = o * lax.rsqrt(ms + RMS_EPS) * ng_ref[...]
    o_ref[...] = (on * gate_ref[...].astype(F32)).astype(BF16)


def _hgrn(hq, hf_f, hf_b, hi, hg_gate, lower, norm_g, batch, seq_len):
    hk = hq.shape[-1]
    tb = HG_BLOCK
    nblk = seq_len // tb
    r3 = lambda a: a.reshape(batch, seq_len, a.shape[-1])
    const = lambda a: pl.BlockSpec(a.shape, lambda b, j: (0,) * a.ndim)
    fwd_blk = pl.BlockSpec((None, tb, hk), lambda b, j: (b, j, 0))
    bwd_blk = pl.BlockSpec((None, tb, hk), lambda b, j: (b, nblk - 1 - j, 0))
    cparams = pltpu.CompilerParams(dimension_semantics=("parallel", "arbitrary"),
                                   vmem_limit_bytes=VMEM_LIMIT_BYTES)
    state = pltpu.VMEM((hk // HG_DIM, HG_DIM, LANES), F32)
    lb_f, lb_b = lower[0:1], lower[1:2]
    consts_f = [jnp.asarray(a, F32) for a in _hgrn_consts(False)]
    consts_b = [jnp.asarray(a, F32) for a in _hgrn_consts(True)]
    head_of = np.arange(HG_SLAB) // HG_DIM
    ones_bd = jnp.asarray(head_of[:, None] == head_of[None, :], BF16)
    o_f = pl.pallas_call(
        _hgrn_fwd_kernel,
        grid=(batch, nblk),
        in_specs=[fwd_blk, fwd_blk, fwd_blk, const(lb_f)] + [const(a) for a in consts_f] + [const(ones_bd)],
        out_specs=fwd_blk,
        out_shape=jax.ShapeDtypeStruct((batch, seq_len, hk), F32),
        scratch_shapes=[state],
        compiler_params=cparams,
        name="hgrn_fwd",
    )(r3(hq), r3(hf_f), r3(hi), lb_f, *consts_f, ones_bd)
    ng = jnp.tile(norm_g.astype(F32), HG_HEADS)[None, :]
    o = pl.pallas_call(
        _hgrn_bwd_kernel,
        grid=(batch, nblk),
        in_specs=[bwd_blk, bwd_blk, bwd_blk, const(lb_b)] + [const(a) for a in consts_b]
                 + [bwd_blk, bwd_blk, const(ng), const(ones_bd)],
        out_specs=bwd_blk,
        out_shape=jax.ShapeDtypeStruct((batch, seq_len, hk), BF16),
        scratch_shapes=[state, pltpu.VMEM((tb, hk), F32)],
        compiler_params=cparams,
        name="hgrn_bwd",
    )(r3(hq), r3(hf_b), r3(hi), lb_b, *consts_b, o_f, r3(hg_gate), ng, ones_bd)
    return o.reshape(batch * seq_len, hk)


def _layer_norm(y, g, b):
    mu = jnp.mean(y, axis=-1, keepdims=True)
    d = y - mu
    var = jnp.mean(d * d, axis=-1, keepdims=True)
    return d * lax.rsqrt(var + LN_EPS) * g + b


def _mix_ffn_kernel(x_ref, oa_ref, oh_ref, ga_ref, gb_ref, wpa_ref, wph_ref, wout_ref, w1_ref, w2_ref,
                    g1_ref, b1_ref, g2_ref, b2_ref, y_ref):
    tm = x_ref.shape[0]
    halves = [slice(0, tm // 2), slice(tm // 2, tm)]

    def merge(r):
        mixed = (ga_ref[r, :].astype(F32) * _dot(oa_ref[r, :], wpa_ref[...])
                 + gb_ref[r, :].astype(F32) * _dot(oh_ref[r, :], wph_ref[...]))
        return ALPHA * x_ref[r, :] + _dot(mixed.astype(BF16), wout_ref[...])

    def hidden(x1):
        h = jnp.maximum(_dot(x1.astype(BF16), w1_ref[...]), 0.0)
        return (h * h).astype(BF16)

    pre = [merge(r) for r in halves]
    x1, hh = [None, None], [None, None]
    for i in range(2):
        x1[i] = _layer_norm(pre[i], g1_ref[...], b1_ref[...])
        hh[i] = hidden(x1[i])
    z = [ALPHA * x1[i] + _dot(hh[i], w2_ref[...]) for i in range(2)]
    for i, r in enumerate(halves):
        y_ref[r, :] = _layer_norm(z[i], g2_ref[...], b2_ref[...])


def _mix_ffn(x2d, o_att, o_hg, ga, gb, wpa, wph, wout, w1, w2, g1, b1, g2, b2):
    n_tok, d_model = x2d.shape
    tm = FFN_ROWS
    row = lambda c: pl.BlockSpec((tm, c), lambda i: (i, 0))
    resident = lambda a: pl.BlockSpec(a.shape, lambda i: (0, 0), pipeline_mode=pl.Buffered(1))
    vec = lambda a: pl.BlockSpec((1, a.shape[-1]), lambda i: (0, 0))
    v2 = lambda a: a.reshape(1, -1).astype(F32)
    return pl.pallas_call(
        _mix_ffn_kernel,
        grid=(n_tok // tm,),
        in_specs=[row(d_model), row(o_att.shape[1]), row(o_hg.shape[1]), row(d_model), row(d_model),
                  resident(wpa), resident(wph), resident(wout), resident(w1), resident(w2),
                  vec(g1), vec(b1), vec(g2), vec(b2)],
        out_specs=row(d_model),
        out_shape=jax.ShapeDtypeStruct((n_tok, d_model), F32),
        compiler_params=pltpu.CompilerParams(dimension_semantics=("parallel",),
                                             vmem_limit_bytes=VMEM_LIMIT_BYTES),
        name="mix_ffn",
    )(x2d, o_att, o_hg, ga, gb, wpa, wph, wout, w1, w2, v2(g1), v2(b1), v2(g2), v2(b2))


def _rope_tables(seq_len):
    half = ROPE_DIM // 2
    inv = ROPE_THETA ** (-jnp.arange(0, ROPE_DIM, 2, dtype=F32) / ROPE_DIM)
    ang = jnp.arange(seq_len, dtype=F32)[:, None] * inv[None, :]
    cos, sin = jnp.cos(ang), jnp.sin(ang)
    ones = jnp.ones((seq_len, HEAD_DIM - ROPE_DIM), F32)
    zeros_rest = jnp.zeros((seq_len, HEAD_DIM - ROPE_DIM), F32)
    zeros_half = jnp.zeros((seq_len, half), F32)
    c = jnp.concatenate([cos, cos, ones], axis=1)
    s1 = jnp.concatenate([zeros_half, sin, zeros_rest], axis=1)
    s2 = jnp.concatenate([-sin, zeros_half, zeros_rest], axis=1)
    rep = LANES // HEAD_DIM
    return tuple(jnp.tile(t, (1, rep)) for t in (c, s1, s2))


def _trunk(x, params, lower, rope_tabs):
    batch, seq_len, d_model = x.shape
    x2d = x.reshape(batch * seq_len, d_model)
    for l in range(DEPTH):
        p = params[l]
        q, k2, v2, hq, hf_f, hf_b, hi, hg_gate, ga, gb = _proj(x2d, p["w_in"], rope_tabs, seq_len, d_model)
        o_att = _attn(q, k2, v2, p["sink"], batch, seq_len)
        o_hg = _hgrn(hq, hf_f, hf_b, hi, hg_gate, lower[l], p["norm_g"], batch, seq_len)
        x2d = _mix_ffn(x2d, o_att, o_hg, ga, gb, p["wpa"], p["wph"], p["wout"], p["w1"], p["w2"],
                       p["g1"], p["b1"], p["g2"], p["b2"])
    return x2d.reshape(batch, seq_len, d_model)


def kernel(x_prompt, x_sample, w_in, att_sink, hgrn_lb, hgrn_norm_g, w_proj_att, w_proj_hgrn, w_out,
           ln1_g, ln1_b, w_ff1, w_ff2, ln2_g, ln2_b):
    sm = jax.nn.softmax(hgrn_lb.astype(F32), axis=0)
    lower = jnp.cumsum(sm, axis=0) - sm[0:1]
    params = []
    for l in range(DEPTH):
        params.append(dict(
            w_in=w_in[l].astype(BF16), sink=att_sink[l].astype(F32), norm_g=hgrn_norm_g[l],
            wpa=w_proj_att[l].astype(BF16), wph=w_proj_hgrn[l].astype(BF16), wout=w_out[l].astype(BF16),
            w1=w_ff1[l].astype(BF16), w2=w_ff2[l].astype(BF16),
            g1=ln1_g[l], b1=ln1_b[l], g2=ln2_g[l], b2=ln2_b[l]))
    rope_tabs = _rope_tables(x_prompt.shape[1])
    y_prompt = _trunk(x_prompt, params, lower, rope_tabs)
    if x_sample.shape[1] != x_prompt.shape[1]:
        rope_tabs = _rope_tables(x_sample.shape[1])
    y_sample = _trunk(x_sample, params, lower, rope_tabs)
    return (y_prompt, y_sample)
```
